```python
import jax, jax.numpy as jnp
from jax import lax
import numpy as np

D_MODEL = 1024
BATCH = 2
SEQ = 16384
DEPTH = 4

N_HEADS = 8
HEAD_DIM = 128
D_ATTN = N_HEADS * HEAD_DIM
Q_BLOCK = 128
FORGET_BIAS = 2.0
D_RNN = D_MODEL
N_RNN_BLOCKS = 16
RNN_BLOCK = D_RNN // N_RNN_BLOCKS
CONV_WIDTH = 4
LRU_C = 8.0
D_FF = 2816
NORM_EPS = 1e-6

SPLIT_POINTS = (
    D_ATTN,
    2 * D_ATTN,
    3 * D_ATTN,
    3 * D_ATTN + N_HEADS,
    3 * D_ATTN + N_HEADS + D_RNN,
    3 * D_ATTN + N_HEADS + 2 * D_RNN,
    3 * D_ATTN + N_HEADS + 2 * D_RNN + D_MODEL,
)
N_IN = 3 * D_ATTN + N_HEADS + 2 * D_RNN + 2 * D_MODEL

kernel_name = "fox_rglru_macaron_hybrid"


def rms_norm(x, g):
    xf = x.astype(jnp.float32)
    y = xf * lax.rsqrt(jnp.mean(xf * xf, axis=-1, keepdims=True) + NORM_EPS)
    return (y * g.astype(jnp.float32)).astype(x.dtype)


def swiglu(x, w_gate, w_up, w_down):
    return (jax.nn.silu(x @ w_gate) * (x @ w_up)) @ w_down


def forgetting_attention(q, k, v, log_f):
    B, S = q.shape[0], q.shape[1]
    nb = S // Q_BLOCK
    scale = HEAD_DIM ** -0.5
    c = jnp.cumsum(log_f, axis=1).transpose(0, 2, 1)
    q = q.transpose(0, 2, 1, 3)
    k = k.transpose(0, 2, 1, 3)
    v = v.transpose(0, 2, 1, 3)
    qb = q.reshape(B, N_HEADS, nb, Q_BLOCK, HEAD_DIM).transpose(2, 0, 1, 3, 4)
    cb = c.reshape(B, N_HEADS, nb, Q_BLOCK).transpose(2, 0, 1, 3)
    k_pos = jnp.arange(S)

    def one_block(args):
        q_i, c_i, i = args
        s = jnp.einsum('bhqd,bhkd->bhqk', q_i, k).astype(jnp.float32) * scale
        s = s + (c_i[..., :, None] - c[:, :, None, :])
        q_pos = i * Q_BLOCK + jnp.arange(Q_BLOCK)
        mask = k_pos[None, :] <= q_pos[:, None]
        s = jnp.where(mask, s, -jnp.inf)
        p = jax.nn.softmax(s, axis=-1)
        return jnp.einsum('bhqk,bhkd->bhqd', p.astype(v.dtype), v)

    out = lax.map(one_block, (qb, cb, jnp.arange(nb)))
    return out.transpose(1, 0, 3, 2, 4).reshape(B, S, D_ATTN)


def causal_depthwise_conv(x, w, b):
    S = x.shape[1]
    xp = jnp.pad(x, ((0, 0), (CONV_WIDTH - 1, 0), (0, 0)))
    y = b
    for tap in range(CONV_WIDTH):
        y = y + xp[:, tap:tap + S, :] * w[tap]
    return y


def block_diag_linear(x, w, b):
    B, S = x.shape[0], x.shape[1]
    xb = x.reshape(B, S, N_RNN_BLOCKS, RNN_BLOCK)
    return jnp.einsum('bsnc,ncd->bsnd', xb, w).reshape(B, S, D_RNN) + b


def rg_lru(x, w_a, b_a, w_x, b_x, lam):
    xf = x.astype(jnp.float32)
    r = jax.nn.sigmoid(block_diag_linear(x, w_a, b_a).astype(jnp.float32))
    i = jax.nn.sigmoid(block_diag_linear(x, w_x, b_x).astype(jnp.float32))
    log_a = -LRU_C * r * jax.nn.softplus(-lam.astype(jnp.float32))
    a = jnp.exp(log_a)
    u = jnp.sqrt(-jnp.expm1(2.0 * log_a)) * (i * xf)

    def combine(left, right):
        a1, b1 = left
        a2, b2 = right
        return a1 * a2, a2 * b1 + b2

    _, h = lax.associative_scan(combine, (a, u), axis=1)
    return h.astype(x.dtype)


def setup_inputs(seed: int = 0) -> dict:
    key = jax.random.key(seed)
    ks = jax.random.split(key, 24)
    L, D, F = DEPTH, D_MODEL, D_FF
    nrm = lambda k, shape, fan_in: jax.random.normal(k, shape, jnp.float32) * (fan_in ** -0.5)
    gain = lambda k, shape: 1.0 + 0.02 * jax.random.normal(k, shape, jnp.float32)
    bias = lambda k, shape: 0.02 * jax.random.normal(k, shape, jnp.float32)

    x = jax.random.normal(ks[0], (BATCH, SEQ, D), jnp.float32)
    b_in = bias(ks[6], (L, N_IN))
    b_in = b_in.at[:, SPLIT_POINTS[2]:SPLIT_POINTS[3]].add(FORGET_BIAS)
    u = jax.random.uniform(ks[15], (L, D_RNN), jnp.float32, 0.9, 0.999)
    lru_lambda = jnp.log(u) - jnp.log1p(-u)
    return {
        "x": x,
        "ffn1_norm": gain(ks[1], (L, D)),
        "ffn1_w_gate": nrm(ks[2], (L, D, F), D),
        "ffn1_w_up": nrm(ks[3], (L, D, F), D),
        "ffn1_w_down": nrm(ks[4], (L, F, D), F),
        "mix_norm": gain(ks[5], (L, D)),
        "w_in": nrm(ks[7], (L, D, N_IN), D),
        "b_in": b_in,
        "q_norm": gain(ks[8], (L, HEAD_DIM)),
        "k_norm": gain(ks[9], (L, HEAD_DIM)),
        "conv_w": nrm(ks[10], (L, CONV_WIDTH, D_RNN), CONV_WIDTH),
        "conv_b": bias(ks[11], (L, D_RNN)),
        "lru_w_a": nrm(ks[12], (L, N_RNN_BLOCKS, RNN_BLOCK, RNN_BLOCK), RNN_BLOCK),
        "lru_b_a": bias(ks[13], (L, D_RNN)),
        "lru_w_x": nrm(ks[14], (L, N_RNN_BLOCKS, RNN_BLOCK, RNN_BLOCK), RNN_BLOCK),
        "lru_b_x": bias(ks[16], (L, D_RNN)),
        "lru_lambda": lru_lambda,
        "w_o_attn": nrm(ks[17], (L, D_ATTN, D), D_ATTN),
        "w_o_rnn": nrm(ks[18], (L, D_RNN, D), D_RNN),
        "w_out": nrm(ks[19], (L, D, D), D),
        "ffn2_norm": gain(ks[20], (L, D)),
        "ffn2_w_gate": nrm(ks[21], (L, D, F), D),
        "ffn2_w_up": nrm(ks[22], (L, D, F), D),
        "ffn2_w_down": nrm(ks[23], (L, F, D), F),
    }


def reference(x, ffn1_norm, ffn1_w_gate, ffn1_w_up, ffn1_w_down, mix_norm, w_in, b_in,
              q_norm, k_norm, conv_w, conv_b, lru_w_a, lru_b_a, lru_w_x, lru_b_x, lru_lambda,
              w_o_attn, w_o_rnn, w_out, ffn2_norm, ffn2_w_gate, ffn2_w_up, ffn2_w_down):
    B, S = x.shape[0], x.shape[1]
    for l in range(DEPTH):
        x = x + 0.5 * swiglu(rms_norm(x, ffn1_norm[l]), ffn1_w_gate[l], ffn1_w_up[l], ffn1_w_down[l])

        h = rms_norm(x, mix_norm[l])
        proj = h @ w_in[l] + b_in[l]
        q, k, v, f_logit, xr, gr, g_attn, g_rnn = jnp.split(proj, SPLIT_POINTS, axis=-1)

        q = rms_norm(q.reshape(B, S, N_HEADS, HEAD_DIM), q_norm[l])
        k = rms_norm(k.reshape(B, S, N_HEADS, HEAD_DIM), k_norm[l])
        v = v.reshape(B, S, N_HEADS, HEAD_DIM)
        log_f = jax.nn.log_sigmoid(f_logit.astype(jnp.float32))
        y_attn = forgetting_attention(q, k, v, log_f) @ w_o_attn[l]

        xr = causal_depthwise_conv(xr, conv_w[l], conv_b[l])
        yr = rg_lru(xr, lru_w_a[l], lru_b_a[l], lru_w_x[l], lru_b_x[l], lru_lambda[l]) * jax.nn.gelu(gr)
        y_rnn = yr @ w_o_rnn[l]

        merged = jax.nn.sigmoid(g_attn) * y_attn + jax.nn.sigmoid(g_rnn) * y_rnn
        x = x + merged @ w_out[l]

        x = x + 0.5 * swiglu(rms_norm(x, ffn2_norm[l]), ffn2_w_gate[l], ffn2_w_up[l], ffn2_w_down[l])
    return x
```

```python
import functools

import jax
import jax.numpy as jnp
from jax import lax
from jax.experimental import pallas as pl
from jax.experimental.pallas import tpu as pltpu

F32 = jnp.float32
BF16 = jnp.bfloat16

D_MODEL = 1024
N_HEADS = 8
HEAD_DIM = 128
D_ATTN = N_HEADS * HEAD_DIM
D_RNN = D_MODEL
N_RNN_BLOCKS = 16
RNN_BLOCK = D_RNN // N_RNN_BLOCKS
CONV_WIDTH = 4
LRU_C = 8.0
D_FF = 2816
NORM_EPS = 1e-6

V7X_LANES = 128
V7X_SUBLANES = 8
V7X_MXU_DIM = 256
V7X_VMEM_BYTES = 64 * 1024 * 1024

AUG_DIM = HEAD_DIM + V7X_LANES
N_SPLIT = 3

FF_CHUNKS = ((0, 1024), (1024, 2048), (2048, D_FF))

TM_FFN = 512
TM_OUT = 256
TM_RNN = 256
T_ATTN = 512

MASK_VALUE = -1e30


def _vmem_limit(mib):
    return min(mib * 1024 * 1024, V7X_VMEM_BYTES - 4 * 1024 * 1024)


def _const_spec(shape):
    nd = len(shape)
    return pl.BlockSpec(shape, lambda *_: (0,) * nd, pipeline_mode=pl.Buffered(1))


def _rms_norm(x, g):
    return x * lax.rsqrt(jnp.mean(x * x, axis=-1, keepdims=True) + NORM_EPS) * g


def _swiglu(h_bf, wg_ref, wu_ref, wd_ref):
    acc = None
    for lo, hi in FF_CHUNKS:
        g = jnp.dot(h_bf, wg_ref[:, lo:hi], preferred_element_type=F32)
        u = jnp.dot(h_bf, wu_ref[:, lo:hi], preferred_element_type=F32)
        a = (g * jax.nn.sigmoid(g) * u).astype(BF16)
        part = jnp.dot(a, wd_ref[lo:hi, :], preferred_element_type=F32)
        acc = part if acc is None else acc + part
    return acc


def _ffn_kernel(x_ref, g_ref, wg_ref, wu_ref, wd_ref, o_ref):
    x = x_ref[...]
    h = _rms_norm(x, g_ref[...]).astype(BF16)
    o_ref[...] = x + 0.5 * _swiglu(h, wg_ref, wu_ref, wd_ref)


def _ffn(x2d, g, wg, wu, wd):
    n_tok = x2d.shape[0]
    tm = TM_FFN
    tile = pl.BlockSpec((tm, D_MODEL), lambda i: (i, 0))
    return pl.pallas_call(
        _ffn_kernel,
        grid=(n_tok // tm,),
        in_specs=[tile, _const_spec(g.shape), _const_spec(wg.shape), _const_spec(wu.shape),
                  _const_spec(wd.shape)],
        out_specs=tile,
        out_shape=jax.ShapeDtypeStruct((n_tok, D_MODEL), F32),
        compiler_params=pltpu.CompilerParams(
            dimension_semantics=("parallel",), vmem_limit_bytes=_vmem_limit(48)),
        name="ffn",
    )(x2d, g, wg, wu, wd)


def _split3(c):
    hi = c.astype(BF16).astype(F32)
    r = c - hi
    mid = r.astype(BF16).astype(F32)
    lo = (r - mid).astype(BF16).astype(F32)
    return hi, mid, lo


def _attn_proj_kernel(x_ref, g_ref, wqkv_ref, bqkv_ref, wft_ref, bf_ref, qn_ref, kn_ref, sel_ref,
                      q_ref, kaug_ref, vt_ref, c_ref, c_carry):
    tm = x_ref.shape[0]

    @pl.when(pl.program_id(1) == 0)
    def _():
        c_carry[...] = jnp.zeros_like(c_carry)

    h = _rms_norm(x_ref[...], g_ref[...]).astype(BF16)
    qkv = jnp.dot(h, wqkv_ref[...], preferred_element_type=F32) + bqkv_ref[...]

    fl = lax.dot_general(wft_ref[...], h, (((1,), (1,)), ((), ())), preferred_element_type=F32)
    fl = fl[:N_HEADS, :] + bf_ref[...]
    log_f = jnp.minimum(fl, 0.0) - jnp.log1p(jnp.exp(-jnp.abs(fl)))

    lane = lax.broadcasted_iota(jnp.int32, log_f.shape, 1)
    cs = log_f
    d = 1
    while d < tm:
        cs = cs + jnp.where(lane >= d, pltpu.roll(cs, d, axis=1), 0.0)
        d *= 2
    c = cs + c_carry[:, 0:1]
    c_carry[...] = jnp.broadcast_to(c[:, tm - 1:tm], c_carry.shape)
    c_ref[...] = c

    hi, mid, lo = _split3(c)
    ones = jnp.ones_like(c)
    pad = jnp.zeros((V7X_LANES - (N_SPLIT + 1) * N_HEADS, tm), F32)
    stack = jnp.concatenate([-hi, -mid, -lo, ones, pad], axis=0)
    extras = jnp.dot(stack.T.astype(BF16), sel_ref[...], preferred_element_type=F32)

    scale = HEAD_DIM ** -0.5
    for hh in range(N_HEADS):
        sl = slice(hh * HEAD_DIM, (hh + 1) * HEAD_DIM)
        qh = qkv[:, sl]
        qh = qh * lax.rsqrt(jnp.mean(qh * qh, axis=-1, keepdims=True) + NORM_EPS) * qn_ref[...]
        q_ref[:, sl] = (qh * scale).astype(BF16)
        kh = qkv[:, D_ATTN + hh * HEAD_DIM:D_ATTN + (hh + 1) * HEAD_DIM]
        kh = kh * lax.rsqrt(jnp.mean(kh * kh, axis=-1, keepdims=True) + NORM_EPS) * kn_ref[...]
        kaug_ref[hh, :, 0:HEAD_DIM] = kh.astype(BF16)
        kaug_ref[hh, :, HEAD_DIM:AUG_DIM] = extras[:, sl].astype(BF16)
        vh = qkv[:, 2 * D_ATTN + hh * HEAD_DIM:2 * D_ATTN + (hh + 1) * HEAD_DIM]
        vt_ref[hh] = vh.T.astype(BF16)


def _attn_proj(x, g, wqkv, bqkv, wft, bf, qn, kn, sel):
    bsz, seq, _ = x.shape
    tm = T_ATTN
    n_t = seq // tm
    out_shape = (
        jax.ShapeDtypeStruct((bsz, seq, D_ATTN), BF16),
        jax.ShapeDtypeStruct((bsz, N_HEADS, seq, AUG_DIM), BF16),
        jax.ShapeDtypeStruct((bsz, N_HEADS, n_t, HEAD_DIM, tm), BF16),
        jax.ShapeDtypeStruct((bsz, N_HEADS, seq), F32),
    )
    out_specs = (
        pl.BlockSpec((None, tm, D_ATTN), lambda b, t: (b, t, 0)),
        pl.BlockSpec((None, N_HEADS, tm, AUG_DIM), lambda b, t: (b, 0, t, 0)),
        pl.BlockSpec((None, N_HEADS, None, HEAD_DIM, tm), lambda b, t: (b, 0, t, 0, 0)),
        pl.BlockSpec((None, N_HEADS, tm), lambda b, t: (b, 0, t)),
    )
    consts = (g, wqkv, bqkv, wft, bf, qn, kn, sel)
    return pl.pallas_call(
        _attn_proj_kernel,
        grid=(bsz, n_t),
        in_specs=[pl.BlockSpec((None, tm, D_MODEL), lambda b, t: (b, t, 0))]
        + [_const_spec(a.shape) for a in consts],
        out_specs=out_specs,
        out_shape=out_shape,
        scratch_shapes=[pltpu.VMEM((N_HEADS, V7X_LANES), F32)],
        compiler_params=pltpu.CompilerParams(
            dimension_semantics=("arbitrary", "arbitrary"), vmem_limit_bytes=_vmem_limit(48)),
        name="attn_proj",
    )(x, *consts)


def _gelu_tanh(x):
    return 0.5 * x * (1.0 + jnp.tanh(0.7978845608028654 * (x + 0.044715 * (x * x * x))))


def _rnn_kernel(x_ref, g_ref, wr_ref, br_ref, cw_ref, cb_ref, wgate_ref, ba_ref, bx_ref, lam_ref,
                wo_ref, sga_ref, rnn_ref, xext, a_scr, u_scr, h_carry):
    tm = x_ref.shape[0]
    sub = V7X_SUBLANES

    @pl.when(pl.program_id(1) == 0)
    def _():
        xext[0:sub, :] = jnp.zeros((sub, D_RNN), F32)
        h_carry[...] = jnp.zeros_like(h_carry)

    h = _rms_norm(x_ref[...], g_ref[...]).astype(BF16)
    rest = jnp.dot(h, wr_ref[...], preferred_element_type=F32) + br_ref[...]
    xr = rest[:, 0:D_RNN]
    gr = rest[:, D_RNN:2 * D_RNN]
    g_attn = rest[:, 2 * D_RNN:2 * D_RNN + D_MODEL]
    g_rnn = rest[:, 2 * D_RNN + D_MODEL:]
    sga_ref[...] = jax.nn.sigmoid(g_attn)

    xext[sub:sub + tm, :] = xr
    conv = cb_ref[...]
    for tap in range(CONV_WIDTH):
        off = sub - (CONV_WIDTH - 1) + tap
        conv = conv + xext[off:off + tm, :] * cw_ref[tap:tap + 1, :]
    xext[0:sub, :] = xext[tm:tm + sub, :]

    conv_bf = conv.astype(BF16)
    ra, ix = [], []
    for j in range(D_RNN // V7X_MXU_DIM):
        gz = jnp.dot(conv_bf[:, j * V7X_MXU_DIM:(j + 1) * V7X_MXU_DIM], wgate_ref[j],
                     preferred_element_type=F32)
        ra.append(gz[:, :V7X_MXU_DIM])
        ix.append(gz[:, V7X_MXU_DIM:])
    r = jax.nn.sigmoid(jnp.concatenate(ra, axis=1) + ba_ref[...])
    i = jax.nn.sigmoid(jnp.concatenate(ix, axis=1) + bx_ref[...])
    nlam = -lam_ref[...]
    softplus = jnp.maximum(nlam, 0.0) + jnp.log1p(jnp.exp(-jnp.abs(nlam)))
    log_a = -LRU_C * r * softplus
    a = jnp.exp(log_a)
    u = jnp.sqrt(-jnp.tanh(log_a) * (a * a + 1.0)) * (i * conv)
    a_scr[...] = a
    u_scr[...] = u

    row = lax.broadcasted_iota(jnp.int32, (sub, D_RNN), 0)

    def group(gi, h_prev):
        r0 = pl.multiple_of(gi * sub, sub)
        aa = a_scr[pl.ds(r0, sub), :]
        bb = u_scr[pl.ds(r0, sub), :]
        d = 1
        while d < sub:
            keep = row >= d
            a_sh = jnp.where(keep, pltpu.roll(aa, d, axis=0), 1.0)
            b_sh = jnp.where(keep, pltpu.roll(bb, d, axis=0), 0.0)
            bb = aa * b_sh + bb
            aa = aa * a_sh
            d *= 2
        hs = aa * h_prev + bb
        u_scr[pl.ds(r0, sub), :] = hs
        return jnp.broadcast_to(hs[sub - 1:sub, :], (sub, D_RNN))

    h_carry[...] = lax.fori_loop(0, tm // sub, group, h_carry[...])

    yr = (u_scr[...] * _gelu_tanh(gr)).astype(BF16)
    y_rnn = jnp.dot(yr, wo_ref[...], preferred_element_type=F32)
    rnn_ref[...] = jax.nn.sigmoid(g_rnn) * y_rnn


def _rnn_branch(x, g, wr, br, cw, cb, wgate, ba, bx, lam, wo):
    bsz, seq, _ = x.shape
    tm = TM_RNN
    tile = pl.BlockSpec((None, tm, D_MODEL), lambda b, t: (b, t, 0))
    consts = (g, wr, br, cw, cb, wgate, ba, bx, lam, wo)
    out = jax.ShapeDtypeStruct((bsz, seq, D_MODEL), F32)
    return pl.pallas_call(
        _rnn_kernel,
        grid=(bsz, seq // tm),
        in_specs=[tile] + [_const_spec(a.shape) for a in consts],
        out_specs=(tile, tile),
        out_shape=(out, out),
        scratch_shapes=[
            pltpu.VMEM((tm + V7X_SUBLANES, D_RNN), F32),
            pltpu.VMEM((tm, D_RNN), F32),
            pltpu.VMEM((tm, D_RNN), F32),
            pltpu.VMEM((V7X_SUBLANES, D_RNN), F32),
        ],
        compiler_params=pltpu.CompilerParams(
            dimension_semantics=("arbitrary", "arbitrary"), vmem_limit_bytes=_vmem_limit(48)),
        name="rnn_branch",
    )(x, *consts)


def _fox_attn_kernel(q_ref, c_ref, k_ref, vt_ref, o_ref, qt_aug, m_s, l_s, acc):
    tq = q_ref.shape[0]
    tk = vt_ref.shape[2]
    hh = pl.program_id(1)
    qi = pl.program_id(2)

    qt_aug[0:HEAD_DIM, :] = q_ref[...].astype(F32).T.astype(BF16)
    c_row = c_ref[pl.ds(hh, 1), :]
    hi, mid, lo = _split3(c_row)
    row = lax.broadcasted_iota(jnp.int32, (V7X_SUBLANES, tq), 0)
    top = jnp.where(row == 0, hi, jnp.where(row == 1, mid, jnp.where(row == 2, lo,
          jnp.where(row < 2 * N_SPLIT, 1.0, 0.0))))
    qt_aug[HEAD_DIM:AUG_DIM, :] = jnp.concatenate(
        [top, jnp.zeros((V7X_LANES - V7X_SUBLANES, tq), F32)], axis=0).astype(BF16)

    m_s[...] = jnp.full(m_s.shape, MASK_VALUE, F32)
    l_s[...] = jnp.zeros_like(l_s)
    acc[...] = jnp.zeros_like(acc)

    def block(j, masked):
        kb = k_ref[pl.ds(pl.multiple_of(j * tk, tk), tk), :]
        st = jnp.dot(kb, qt_aug[...], preferred_element_type=F32)
        if masked:
            krow = lax.broadcasted_iota(jnp.int32, st.shape, 0)
            qcol = lax.broadcasted_iota(jnp.int32, st.shape, 1)
            st = jnp.where(krow <= qcol, st, MASK_VALUE)
        m_old = m_s[...]
        m_new = jnp.maximum(m_old, jnp.max(st, axis=0, keepdims=True))
        alpha = jnp.exp(m_old - m_new)
        p = jnp.exp(st - m_new)
        l_s[...] = alpha * l_s[...] + jnp.sum(p, axis=0, keepdims=True)
        acc[...] = alpha * acc[...] + jnp.dot(vt_ref[j], p.astype(BF16), preferred_element_type=F32)
        m_s[...] = m_new

    def body(j, carry):
        block(j, False)
        return carry

    lax.fori_loop(0, qi, body, 0)
    block(qi, True)
    o_ref[...] = (acc[...] / l_s[...]).T.astype(BF16)


def _fox_attn(q, c, kaug, vt):
    bsz, seq, _ = q.shape
    n_t, tk = vt.shape[2], vt.shape[4]
    tq = tk
    return pl.pallas_call(
        _fox_attn_kernel,
        grid=(bsz, N_HEADS, seq // tq),
        in_specs=[
            pl.BlockSpec((None, tq, HEAD_DIM), lambda b, h, i: (b, i, h)),
            pl.BlockSpec((None, N_HEADS, tq), lambda b, h, i: (b, 0, i)),
            pl.BlockSpec((None, None, seq, AUG_DIM), lambda b, h, i: (b, h, 0, 0)),
            pl.BlockSpec((None, None, n_t, HEAD_DIM, tk), lambda b, h, i: (b, h, 0, 0, 0)),
        ],
        out_specs=pl.BlockSpec((None, tq, HEAD_DIM), lambda b, h, i: (b, i, h)),
        out_shape=jax.ShapeDtypeStruct((bsz, seq, D_ATTN), BF16),
        scratch_shapes=[
            pltpu.VMEM((AUG_DIM, tq), BF16),
            pltpu.VMEM((1, tq), F32),
            pltpu.VMEM((1, tq), F32),
            pltpu.VMEM((HEAD_DIM, tq), F32),
        ],
        compiler_params=pltpu.CompilerParams(
            dimension_semantics=("arbitrary", "arbitrary", "arbitrary"),
            vmem_limit_bytes=_vmem_limit(48)),
        name="fox_attn",
    )(q, c, kaug, vt)


def _out_ffn_kernel(x_ref, attn_ref, sga_ref, rnn_ref, woa_ref, wout_ref, g_ref, wg_ref, wu_ref,
                    wd_ref, o_ref):
    y_attn = jnp.dot(attn_ref[...], woa_ref[...], preferred_element_type=F32)
    merged = sga_ref[...] * y_attn + rnn_ref[...]
    x1 = x_ref[...] + jnp.dot(merged.astype(BF16), wout_ref[...], preferred_element_type=F32)
    h = _rms_norm(x1, g_ref[...]).astype(BF16)
    o_ref[...] = x1 + 0.5 * _swiglu(h, wg_ref, wu_ref, wd_ref)


def _out_ffn(x2d, attn2d, sga2d, rnn2d, woa, wout, g, wg, wu, wd):
    n_tok = x2d.shape[0]
    tm = TM_OUT
    tile = pl.BlockSpec((tm, D_MODEL), lambda i: (i, 0))
    consts = (woa, wout, g, wg, wu, wd)
    return pl.pallas_call(
        _out_ffn_kernel,
        grid=(n_tok // tm,),
        in_specs=[tile, tile, tile, tile] + [_const_spec(a.shape) for a in consts],
        out_specs=tile,
        out_shape=jax.ShapeDtypeStruct((n_tok, D_MODEL), F32),
        compiler_params=pltpu.CompilerParams(
            dimension_semantics=("parallel",), vmem_limit_bytes=_vmem_limit(56)),
        name="out_ffn",
    )(x2d, attn2d, sga2d, rnn2d, *consts)


def _block_diag_tiles(w):
    per = V7X_MXU_DIM // RNN_BLOCK
    w4 = w.reshape(N_RNN_BLOCKS // per, per, RNN_BLOCK, RNN_BLOCK)
    eye = jnp.eye(per, dtype=w.dtype)
    return jnp.einsum("jarc,ab->jarbc", w4, eye).reshape(-1, V7X_MXU_DIM, V7X_MXU_DIM)


def _bias_selector():
    r = jnp.arange(V7X_LANES)[:, None]
    col = jnp.arange(N_HEADS * V7X_LANES)[None, :]
    head, pos = col // V7X_LANES, col % V7X_LANES
    ones_row = N_SPLIT * N_HEADS + head
    piece_row = (pos - N_SPLIT) * N_HEADS + head
    hit = jnp.where(pos < N_SPLIT, r == ones_row, (pos < 2 * N_SPLIT) & (r == piece_row))
    return hit.astype(BF16)


def kernel(x, ffn1_norm, ffn1_w_gate, ffn1_w_up, ffn1_w_down, mix_norm, w_in, b_in, q_norm, k_norm, conv_w, conv_b, lru_w_a, lru_b_a, lru_w_x, lru_b_x, lru_lambda, w_o_attn, w_o_rnn, w_out, ffn2_norm, ffn2_w_gate, ffn2_w_up, ffn2_w_down):
    bsz, seq, _ = x.shape
    depth = w_in.shape[0]
    n_tok = bsz * seq
    assert seq % T_ATTN == 0 and seq % TM_RNN == 0 and n_tok % TM_FFN == 0 and n_tok % TM_OUT == 0
    row = lambda v: v.reshape(1, -1).astype(F32)
    sel = _bias_selector()
    f_lo, f_hi = 3 * D_ATTN, 3 * D_ATTN + N_HEADS

    for l in range(depth):
        x2d = _ffn(x.reshape(n_tok, D_MODEL), row(ffn1_norm[l]), ffn1_w_gate[l].astype(BF16),
                   ffn1_w_up[l].astype(BF16), ffn1_w_down[l].astype(BF16))
        x = x2d.reshape(bsz, seq, D_MODEL)

        w_l, b_l = w_in[l], b_in[l]
        wft = jnp.zeros((2 * V7X_SUBLANES, D_MODEL), BF16).at[:N_HEADS].set(
            w_l[:, f_lo:f_hi].T.astype(BF16))
        q, kaug, vt, c = _attn_proj(
            x, row(mix_norm[l]), w_l[:, :f_lo].astype(BF16), row(b_l[:f_lo]), wft,
            b_l[f_lo:f_hi].reshape(N_HEADS, 1), row(q_norm[l]), row(k_norm[l]), sel)

        wgate = jnp.concatenate(
            [_block_diag_tiles(lru_w_a[l]), _block_diag_tiles(lru_w_x[l])], axis=2).astype(BF16)
        sga, rnn = _rnn_branch(
            x, row(mix_norm[l]), w_l[:, f_hi:].astype(BF16), row(b_l[f_hi:]), conv_w[l].astype(F32),
            row(conv_b[l]), wgate, row(lru_b_a[l]), row(lru_b_x[l]), row(lru_lambda[l]),
            w_o_rnn[l].astype(BF16))

        attn = _fox_attn(q, c, kaug, vt)

        x2d = _out_ffn(
            x.reshape(n_tok, D_MODEL), attn.reshape(n_tok, D_ATTN), sga.reshape(n_tok, D_MODEL),
            rnn.reshape(n_tok, D_MODEL), w_o_attn[l].astype(BF16), w_out[l].astype(BF16),
            row(ffn2_norm[l]), ffn2_w_gate[l].astype(BF16), ffn2_w_up[l].astype(BF16),
            ffn2_w_down[l].astype(BF16))
        x = x2d.reshape(bsz, seq, D_MODEL)
    return x
```

```python
import functools

import jax
import jax.numpy as jnp
from jax import lax
from jax.experimental import pallas as pl
from jax.experimental.pallas import tpu as pltpu

F32 = jnp.float32
BF16 = jnp.bfloat16

D_MODEL = 1024
N_HEADS = 8
HEAD_DIM = 128
D_ATTN = N_HEADS * HEAD_DIM
D_RNN = D_MODEL
N_RNN_BLOCKS = 16
RNN_BLOCK = D_RNN // N_RNN_BLOCKS
CONV_WIDTH = 4
LRU_C = 8.0
D_FF = 2816
NORM_EPS = 1e-6

V7X_LANES = 128
V7X_SUBLANES = 8
V7X_MXU_DIM = 256
V7X_VMEM_BYTES = 64 * 1024 * 1024

AUG_DIM = HEAD_DIM + V7X_LANES
N_SPLIT = 3

FF_CHUNKS = ((0, 1024), (1024, 2048), (2048, D_FF))

TM_FFN = 512
TM_OUT = 256
TM_RNN = 256
T_ATTN = 512
KV_PER_Q = 4
TQ_ATTN = KV_PER_Q * T_ATTN
VT_ROWS = HEAD_DIM + 16

MASK_VALUE = -1e30
LOG2_E = 1.4426950408889634


def _vmem_limit(mib):
    return min(mib * 1024 * 1024, V7X_VMEM_BYTES - 4 * 1024 * 1024)


def _const_spec(shape):
    nd = len(shape)
    return pl.BlockSpec(shape, lambda *_: (0,) * nd, pipeline_mode=pl.Buffered(1))


def _rms_norm(x, g):
    return x * lax.rsqrt(jnp.mean(x * x, axis=-1, keepdims=True) + NORM_EPS) * g


def _swiglu(h_bf, wg_ref, wu_ref, wd_ref):
    acc = None
    for lo, hi in FF_CHUNKS:
        g = jnp.dot(h_bf, wg_ref[:, lo:hi], preferred_element_type=F32)
        u = jnp.dot(h_bf, wu_ref[:, lo:hi], preferred_element_type=F32)
        a = (g * jax.nn.sigmoid(g) * u).astype(BF16)
        part = jnp.dot(a, wd_ref[lo:hi, :], preferred_element_type=F32)
        acc = part if acc is None else acc + part
    return acc


def _ffn_kernel(x_ref, g_ref, wg_ref, wu_ref, wd_ref, o_ref):
    x = x_ref[...]
    h = _rms_norm(x, g_ref[...]).astype(BF16)
    o_ref[...] = x + 0.5 * _swiglu(h, wg_ref, wu_ref, wd_ref)


def _ffn(x2d, g, wg, wu, wd):
    n_tok = x2d.shape[0]
    tm = TM_FFN
    tile = pl.BlockSpec((tm, D_MODEL), lambda i: (i, 0))
    return pl.pallas_call(
        _ffn_kernel,
        grid=(n_tok // tm,),
        in_specs=[tile, _const_spec(g.shape), _const_spec(wg.shape), _const_spec(wu.shape),
                  _const_spec(wd.shape)],
        out_specs=tile,
        out_shape=jax.ShapeDtypeStruct((n_tok, D_MODEL), F32),
        compiler_params=pltpu.CompilerParams(
            dimension_semantics=("parallel",), vmem_limit_bytes=_vmem_limit(48)),
        name="ffn",
    )(x2d, g, wg, wu, wd)


def _split3(c):
    hi = c.astype(BF16).astype(F32)
    r = c - hi
    mid = r.astype(BF16).astype(F32)
    lo = (r - mid).astype(BF16).astype(F32)
    return hi, mid, lo


def _attn_proj_kernel(x_ref, g_ref, wqkv_ref, bqkv_ref, wft_ref, bf_ref, qn_ref, kn_ref, sel_ref,
                      q_ref, kaug_ref, vt_ref, c_ref, c_carry):
    tm = x_ref.shape[0]

    @pl.when(pl.program_id(1) == 0)
    def _():
        c_carry[...] = jnp.zeros_like(c_carry)

    h = _rms_norm(x_ref[...], g_ref[...]).astype(BF16)
    qkv = jnp.dot(h, wqkv_ref[...], preferred_element_type=F32) + bqkv_ref[...]

    fl = lax.dot_general(wft_ref[...], h, (((1,), (1,)), ((), ())), preferred_element_type=F32)
    fl = fl[:N_HEADS, :] + bf_ref[...]
    log_f = jnp.minimum(fl, 0.0) - jnp.log1p(jnp.exp(-jnp.abs(fl)))

    lane = lax.broadcasted_iota(jnp.int32, log_f.shape, 1)
    cs = log_f
    d = 1
    while d < tm:
        cs = cs + jnp.where(lane >= d, pltpu.roll(cs, d, axis=1), 0.0)
        d *= 2
    c = cs + c_carry[:, 0:1]
    c_carry[...] = jnp.broadcast_to(c[:, tm - 1:tm], c_carry.shape)
    c_ref[...] = c

    hi, mid, lo = _split3(c * LOG2_E)
    ones = jnp.ones_like(c)
    pad = jnp.zeros((V7X_LANES - (N_SPLIT + 1) * N_HEADS, tm), F32)
    stack = jnp.concatenate([-hi, -mid, -lo, ones, pad], axis=0)
    extras = jnp.dot(stack.T.astype(BF16), sel_ref[...], preferred_element_type=F32)

    scale = HEAD_DIM ** -0.5 * LOG2_E
    for hh in range(N_HEADS):
        sl = slice(hh * HEAD_DIM, (hh + 1) * HEAD_DIM)
        qh = qkv[:, sl]
        qh = qh * lax.rsqrt(jnp.mean(qh * qh, axis=-1, keepdims=True) + NORM_EPS) * qn_ref[...]
        q_ref[:, sl] = (qh * scale).astype(BF16)
        kh = qkv[:, D_ATTN + hh * HEAD_DIM:D_ATTN + (hh + 1) * HEAD_DIM]
        kh = kh * lax.rsqrt(jnp.mean(kh * kh, axis=-1, keepdims=True) + NORM_EPS) * kn_ref[...]
        kaug_ref[hh, :, 0:HEAD_DIM] = kh.astype(BF16)
        kaug_ref[hh, :, HEAD_DIM:AUG_DIM] = extras[:, sl].astype(BF16)
        vh = qkv[:, 2 * D_ATTN + hh * HEAD_DIM:2 * D_ATTN + (hh + 1) * HEAD_DIM]
        vt_ref[hh, 0:HEAD_DIM, :] = vh.T.astype(BF16)
        vt_ref[hh, HEAD_DIM:VT_ROWS, :] = jnp.ones((VT_ROWS - HEAD_DIM, tm), BF16)


def _attn_proj(x, g, wqkv, bqkv, wft, bf, qn, kn, sel):
    bsz, seq, _ = x.shape
    tm = T_ATTN
    n_t = seq // tm
    out_shape = (
        jax.ShapeDtypeStruct((bsz, seq, D_ATTN), BF16),
        jax.ShapeDtypeStruct((bsz, N_HEADS, seq, AUG_DIM), BF16),
        jax.ShapeDtypeStruct((bsz, N_HEADS, n_t, VT_ROWS, tm), BF16),
        jax.ShapeDtypeStruct((bsz, N_HEADS, seq), F32),
    )
    out_specs = (
        pl.BlockSpec((None, tm, D_ATTN), lambda b, t: (b, t, 0)),
        pl.BlockSpec((None, N_HEADS, tm, AUG_DIM), lambda b, t: (b, 0, t, 0)),
        pl.BlockSpec((None, N_HEADS, None, VT_ROWS, tm), lambda b, t: (b, 0, t, 0, 0)),
        pl.BlockSpec((None, N_HEADS, tm), lambda b, t: (b, 0, t)),
    )
    consts = (g, wqkv, bqkv, wft, bf, qn, kn, sel)
    return pl.pallas_call(
        _attn_proj_kernel,
        grid=(bsz, n_t),
        in_specs=[pl.BlockSpec((None, tm, D_MODEL), lambda b, t: (b, t, 0))]
        + [_const_spec(a.shape) for a in consts],
        out_specs=out_specs,
        out_shape=out_shape,
        scratch_shapes=[pltpu.VMEM((N_HEADS, V7X_LANES), F32)],
        compiler_params=pltpu.CompilerParams(
            dimension_semantics=("arbitrary", "arbitrary"), vmem_limit_bytes=_vmem_limit(48)),
        name="attn_proj",
    )(x, *consts)


def _gelu_tanh(x):
    return 0.5 * x * (1.0 + jnp.tanh(0.7978845608028654 * (x + 0.044715 * (x * x * x))))


def _rnn_kernel(x_ref, g_ref, wr_ref, br_ref, cw_ref, cb_ref, wgate_ref, ba_ref, bx_ref, lam_ref,
                wo_ref, sga_ref, rnn_ref, xext, a_scr, u_scr, h_carry):
    tm = x_ref.shape[0]
    sub = V7X_SUBLANES

    @pl.when(pl.program_id(1) == 0)
    def _():
        xext[0:sub, :] = jnp.zeros((sub, D_RNN), F32)
        h_carry[...] = jnp.zeros_like(h_carry)

    h = _rms_norm(x_ref[...], g_ref[...]).astype(BF16)
    rest = jnp.dot(h, wr_ref[...], preferred_element_type=F32) + br_ref[...]
    xr = rest[:, 0:D_RNN]
    gr = rest[:, D_RNN:2 * D_RNN]
    g_attn = rest[:, 2 * D_RNN:2 * D_RNN + D_MODEL]
    g_rnn = rest[:, 2 * D_RNN + D_MODEL:]
    sga_ref[...] = jax.nn.sigmoid(g_attn)

    xext[sub:sub + tm, :] = xr
    conv = cb_ref[...]
    for tap in range(CONV_WIDTH):
        off = sub - (CONV_WIDTH - 1) + tap
        conv = conv + xext[off:off + tm, :] * cw_ref[tap:tap + 1, :]
    xext[0:sub, :] = xext[tm:tm + sub, :]

    conv_bf = conv.astype(BF16)
    ra, ix = [], []
    for j in range(D_RNN // V7X_MXU_DIM):
        gz = jnp.dot(conv_bf[:, j * V7X_MXU_DIM:(j + 1) * V7X_MXU_DIM], wgate_ref[j],
                     preferred_element_type=F32)
        ra.append(gz[:, :V7X_MXU_DIM])
        ix.append(gz[:, V7X_MXU_DIM:])
    r = jax.nn.sigmoid(jnp.concatenate(ra, axis=1) + ba_ref[...])
    i = jax.nn.sigmoid(jnp.concatenate(ix, axis=1) + bx_ref[...])
    nlam = -lam_ref[...]
    softplus = jnp.maximum(nlam, 0.0) + jnp.log1p(jnp.exp(-jnp.abs(nlam)))
    log_a = -LRU_C * r * softplus
    a = jnp.exp(log_a)
    u = jnp.sqrt(-jnp.tanh(log_a) * (a * a + 1.0)) * (i * conv)
    a_scr[...] = a
    u_scr[...] = u

    row = lax.broadcasted_iota(jnp.int32, (sub, D_RNN), 0)

    def group(gi, h_prev):
        r0 = pl.multiple_of(gi * sub, sub)
        aa = a_scr[pl.ds(r0, sub), :]
        bb = u_scr[pl.ds(r0, sub), :]
        d = 1
        while d < sub:
            keep = row >= d
            a_sh = jnp.where(keep, pltpu.roll(aa, d, axis=0), 1.0)
            b_sh = jnp.where(keep, pltpu.roll(bb, d, axis=0), 0.0)
            bb = aa * b_sh + bb
            aa = aa * a_sh
            d *= 2
        hs = aa * h_prev + bb
        u_scr[pl.ds(r0, sub), :] = hs
        return jnp.broadcast_to(hs[sub - 1:sub, :], (sub, D_RNN))

    h_carry[...] = lax.fori_loop(0, tm // sub, group, h_carry[...])

    yr = (u_scr[...] * _gelu_tanh(gr)).astype(BF16)
    y_rnn = jnp.dot(yr, wo_ref[...], preferred_element_type=F32)
    rnn_ref[...] = jax.nn.sigmoid(g_rnn) * y_rnn


def _rnn_branch(x, g, wr, br, cw, cb, wgate, ba, bx, lam, wo):
    bsz, seq, _ = x.shape
    tm = TM_RNN
    tile = pl.BlockSpec((None, tm, D_MODEL), lambda b, t: (b, t, 0))
    consts = (g, wr, br, cw, cb, wgate, ba, bx, lam, wo)
    out = jax.ShapeDtypeStruct((bsz, seq, D_MODEL), F32)
    return pl.pallas_call(
        _rnn_kernel,
        grid=(bsz, seq // tm),
        in_specs=[tile] + [_const_spec(a.shape) for a in consts],
        out_specs=(tile, tile),
        out_shape=(out, out),
        scratch_shapes=[
            pltpu.VMEM((tm + V7X_SUBLANES, D_RNN), F32),
            pltpu.VMEM((tm, D_RNN), F32),
            pltpu.VMEM((tm, D_RNN), F32),
            pltpu.VMEM((V7X_SUBLANES, D_RNN), F32),
        ],
        compiler_params=pltpu.CompilerParams(
            dimension_semantics=("arbitrary", "arbitrary"), vmem_limit_bytes=_vmem_limit(48)),
        name="rnn_branch",
    )(x, *consts)


def _fox_attn_kernel(q_ref, c_ref, k_ref, vt_ref, o_ref, qt_aug, st0, st1, p0, p1, al0, al1,
                     m_s, acc):
    tq = q_ref.shape[0]
    tk = vt_ref.shape[2]
    hh = pl.program_id(1)
    qi = pl.program_id(2)
    st, pb, al = (st0, st1), (p0, p1), (al0, al1)

    qt_aug[0:HEAD_DIM, :] = q_ref[...].astype(F32).T.astype(BF16)
    hi, mid, lo = _split3(c_ref[pl.ds(hh, 1), :] * LOG2_E)
    row = lax.broadcasted_iota(jnp.int32, (V7X_SUBLANES, tq), 0)
    top = jnp.where(row == 0, hi, jnp.where(row == 1, mid, jnp.where(row == 2, lo,
          jnp.where(row < 2 * N_SPLIT, 1.0, 0.0))))
    qt_aug[HEAD_DIM:AUG_DIM, :] = jnp.concatenate(
        [top, jnp.zeros((V7X_LANES - V7X_SUBLANES, tq), F32)], axis=0).astype(BF16)

    m_s[...] = jnp.full(m_s.shape, MASK_VALUE, F32)
    acc[...] = jnp.zeros_like(acc)
    p1[...] = jnp.zeros_like(p1)
    al1[...] = jnp.ones_like(al1)

    def score(j, s, c0=0):
        kb = k_ref[pl.ds(pl.multiple_of(j * tk, tk), tk), :]
        st[s][:, c0:] = jnp.dot(kb, qt_aug[:, c0:], preferred_element_type=F32)

    def softmax(s, c0=0, diagonal=False):
        st_ref = st[s]
        if diagonal:
            shape = (tk, tq - c0)
            visible = (lax.broadcasted_iota(jnp.int32, shape, 0)
                       <= lax.broadcasted_iota(jnp.int32, shape, 1))
            st_ref[:, c0:] = jnp.where(visible, st_ref[:, c0:], MASK_VALUE)
        m_old = m_s[:, c0:]
        m_new = jnp.maximum(m_old, jnp.max(st_ref[:, c0:], axis=0, keepdims=True))
        m_s[:, c0:] = m_new
        al[s][:, c0:] = jnp.exp2(m_old - m_new)
        pb[s][:, c0:] = jnp.exp2(st_ref[:, c0:] - m_new).astype(BF16)

    def pv(j, s, c0=0):
        acc[:, c0:] = al[s][:, c0:] * acc[:, c0:] + jnp.dot(
            vt_ref[j], pb[s][:, c0:], preferred_element_type=F32)

    score(0, 0)

    def trip(t, carry):
        a = KV_PER_Q * t
        for d in range(KV_PER_Q):
            s = d % 2
            score(a + d + 1, 1 - s)
            softmax(s)
            pv(jnp.maximum(a + d - 1, 0), 1 - s)
        return carry

    lax.fori_loop(0, qi, trip, 0)
    a = KV_PER_Q * qi
    for d in range(KV_PER_Q):
        s = d % 2
        if d + 1 < KV_PER_Q:
            score(a + d + 1, 1 - s, c0=(d + 1) * tk)
        softmax(s, c0=d * tk, diagonal=True)
        pv(jnp.maximum(a + d - 1, 0), 1 - s, c0=max(d - 1, 0) * tk)
    pv(a + KV_PER_Q - 1, (KV_PER_Q - 1) % 2, c0=(KV_PER_Q - 1) * tk)
    o_ref[...] = (acc[0:HEAD_DIM, :] / acc[HEAD_DIM:HEAD_DIM + 1, :]).T.astype(BF16)


def _fox_attn(q, c, kaug, vt):
    bsz, seq, _ = q.shape
    n_t, tk = vt.shape[2], vt.shape[4]
    tq = TQ_ATTN
    assert tq == KV_PER_Q * tk and KV_PER_Q % 2 == 0 and seq % tq == 0
    return pl.pallas_call(
        _fox_attn_kernel,
        grid=(bsz, N_HEADS, seq // tq),
        in_specs=[
            pl.BlockSpec((None, tq, HEAD_DIM), lambda b, h, i: (b, i, h)),
            pl.BlockSpec((None, N_HEADS, tq), lambda b, h, i: (b, 0, i)),
            pl.BlockSpec((None, None, seq, AUG_DIM), lambda b, h, i: (b, h, 0, 0)),
            pl.BlockSpec((None, None, n_t, VT_ROWS, tk), lambda b, h, i: (b, h, 0, 0, 0)),
        ],
        out_specs=pl.BlockSpec((None, tq, HEAD_DIM), lambda b, h, i: (b, i, h)),
        out_shape=jax.ShapeDtypeStruct((bsz, seq, D_ATTN), BF16),
        scratch_shapes=[
            pltpu.VMEM((AUG_DIM, tq), BF16),
            pltpu.VMEM((tk, tq), F32),
            pltpu.VMEM((tk, tq), F32),
            pltpu.VMEM((tk, tq), BF16),
            pltpu.VMEM((tk, tq), BF16),
            pltpu.VMEM((1, tq), F32),
            pltpu.VMEM((1, tq), F32),
            pltpu.VMEM((1, tq), F32),
            pltpu.VMEM((VT_ROWS, tq), F32),
        ],
        compiler_params=pltpu.CompilerParams(
            dimension_semantics=("arbitrary", "arbitrary", "arbitrary"),
            vmem_limit_bytes=_vmem_limit(56)),
        name="fox_attn",
    )(q, c, kaug, vt)


def _out_ffn_kernel(x_ref, attn_ref, sga_ref, rnn_ref, woa_ref, wout_ref, g_ref, wg_ref, wu_ref,
                    wd_ref, o_ref):
    y_attn = jnp.dot(attn_ref[...], woa_ref[...], preferred_element_type=F32)
    merged = sga_ref[...] * y_attn + rnn_ref[...]
    x1 = x_ref[...] + jnp.dot(merged.astype(BF16), wout_ref[...], preferred_element_type=F32)
    h = _rms_norm(x1, g_ref[...]).astype(BF16)
    o_ref[...] = x1 + 0.5 * _swiglu(h, wg_ref, wu_ref, wd_ref)


def _out_ffn(x2d, attn2d, sga2d, rnn2d, woa, wout, g, wg, wu, wd):
    n_tok = x2d.shape[0]
    tm = TM_OUT
    tile = pl.BlockSpec((tm, D_MODEL), lambda i: (i, 0))
    consts = (woa, wout, g, wg, wu, wd)
    return pl.pallas_call(
        _out_ffn_kernel,
        grid=(n_tok // tm,),
        in_specs=[tile, tile, tile, tile] + [_const_spec(a.shape) for a in consts],
        out_specs=tile,
        out_shape=jax.ShapeDtypeStruct((n_tok, D_MODEL), F32),
        compiler_params=pltpu.CompilerParams(
            dimension_semantics=("parallel",), vmem_limit_bytes=_vmem_limit(56)),
        name="out_ffn",
    )(x2d, attn2d, sga2d, rnn2d, *consts)


def _block_diag_tiles(w):
    per = V7X_MXU_DIM // RNN_BLOCK
    w4 = w.reshape(N_RNN_BLOCKS // per, per, RNN_BLOCK, RNN_BLOCK)
    eye = jnp.eye(per, dtype=w.dtype)
    return jnp.einsum("jarc,ab->jarbc", w4, eye).reshape(-1, V7X_MXU_DIM, V7X_MXU_DIM)


def _bias_selector():
    r = jnp.arange(V7X_LANES)[:, None]
    col = jnp.arange(N_HEADS * V7X_LANES)[None, :]
    head, pos = col // V7X_LANES, col % V7X_LANES
    ones_row = N_SPLIT * N_HEADS + head
    piece_row = (pos - N_SPLIT) * N_HEADS + head
    hit = jnp.where(pos < N_SPLIT, r == ones_row, (pos < 2 * N_SPLIT) & (r == piece_row))
    return hit.astype(BF16)


def kernel(x, ffn1_norm, ffn1_w_gate, ffn1_w_up, ffn1_w_down, mix_norm, w_in, b_in, q_norm, k_norm, conv_w, conv_b, lru_w_a, lru_b_a, lru_w_x, lru_b_x, lru_lambda, w_o_attn, w_o_rnn, w_out, ffn2_norm, ffn2_w_gate, ffn2_w_up, ffn2_w_down):
    bsz, seq, _ = x.shape
    depth = w_in.shape[0]
    n_tok = bsz * seq
    assert seq % T_ATTN == 0 and seq % TM_RNN == 0 and n_tok % TM_FFN == 0 and n_tok % TM_OUT == 0
    row = lambda v: v.reshape(1, -1).astype(F32)
    sel = _bias_selector()
    f_lo, f_hi = 3 * D_ATTN, 3 * D_ATTN + N_HEADS

    for l in range(depth):
        x2d = _ffn(x.reshape(n_tok, D_MODEL), row(ffn1_norm[l]), ffn1_w_gate[l].astype(BF16),
                   ffn1_w_up[l].astype(BF16), ffn1_w_down[l].astype(BF16))
        x = x2d.reshape(bsz, seq, D_MODEL)

        w_l, b_l = w_in[l], b_in[l]
        wft = jnp.zeros((2 * V7X_SUBLANES, D_MODEL), BF16).at[:N_HEADS].set(
            w_l[:, f_lo:f_hi].T.astype(BF16))
        q, kaug, vt, c = _attn_proj(
            x, row(mix_norm[l]), w_l[:, :f_lo].astype(BF16), row(b_l[:f_lo]), wft,
            b_l[f_lo:f_hi].reshape(N_HEADS, 1), row(q_norm[l]), row(k_norm[l]), sel)

        wgate = jnp.concatenate(
            [_block_diag_tiles(lru_w_a[l]), _block_diag_tiles(lru_w_x[l])], axis=2).astype(BF16)
        sga, rnn = _rnn_branch(
            x, row(mix_norm[l]), w_l[:, f_hi:].astype(BF16), row(b_l[f_hi:]), conv_w[l].astype(F32),
            row(conv_b[l]), wgate, row(lru_b_a[l]), row(lru_b_x[l]), row(lru_lambda[l]),
            w_o_rnn[l].astype(BF16))

        attn = _fox_attn(q, c, kaug, vt)

        x2d = _out_ffn(
            x.reshape(n_tok, D_MODEL), attn.reshape(n_tok, D_ATTN), sga.reshape(n_tok, D_MODEL),
            rnn.reshape(n_tok, D_MODEL), w_o_attn[l].astype(BF16), w_out[l].astype(BF16),
            row(ffn2_norm[l]), ffn2_w_gate[l].astype(BF16), ffn2_w_up[l].astype(BF16),
            ffn2_w_down[l].astype(BF16))
        x = x2d.reshape(bsz, seq, D_MODEL)
    return x
```

```python
import functools

import jax
import jax.numpy as jnp
from jax import lax
from jax.experimental import pallas as pl
from jax.experimental.pallas import tpu as pltpu

F32 = jnp.float32
BF16 = jnp.bfloat16

D_MODEL = 1024
N_HEADS = 8
HEAD_DIM = 128
D_ATTN = N_HEADS * HEAD_DIM
D_RNN = D_MODEL
N_RNN_BLOCKS = 16
RNN_BLOCK = D_RNN // N_RNN_BLOCKS
CONV_WIDTH = 4
LRU_C = 8.0
D_FF = 2816
NORM_EPS = 1e-6

V7X_LANES = 128
V7X_SUBLANES = 8
V7X_MXU_DIM = 256
V7X_VMEM_BYTES = 64 * 1024 * 1024

AUG_DIM = HEAD_DIM + V7X_LANES
N_SPLIT = 3

FF_CHUNKS = ((0, 1024), (1024, 2048), (2048, D_FF))

TM_FFN = 512
TM_OUT = 256
TM_RNN = 256
T_ATTN = 512
KV_PER_Q = 4
TQ_ATTN = KV_PER_Q * T_ATTN
VT_ROWS = HEAD_DIM + 16

MASK_VALUE = -1e30
LOG2_E = 1.4426950408889634


def _vmem_limit(mib):
    return min(mib * 1024 * 1024, V7X_VMEM_BYTES - 4 * 1024 * 1024)


def _const_spec(shape):
    nd = len(shape)
    return pl.BlockSpec(shape, lambda *_: (0,) * nd, pipeline_mode=pl.Buffered(1))


def _rms_norm(x, g):
    return x * lax.rsqrt(jnp.mean(x * x, axis=-1, keepdims=True) + NORM_EPS) * g


def _swiglu(h_bf, wg_ref, wu_ref, wd_ref):
    acc = None
    for lo, hi in FF_CHUNKS:
        g = jnp.dot(h_bf, wg_ref[:, lo:hi], preferred_element_type=F32)
        u = jnp.dot(h_bf, wu_ref[:, lo:hi], preferred_element_type=F32)
        a = (g * jax.nn.sigmoid(g) * u).astype(BF16)
        part = jnp.dot(a, wd_ref[lo:hi, :], preferred_element_type=F32)
        acc = part if acc is None else acc + part
    return acc


def _ffn_kernel(x_ref, g_ref, wg_ref, wu_ref, wd_ref, o_ref):
    x = x_ref[...]
    h = _rms_norm(x, g_ref[...]).astype(BF16)
    o_ref[...] = x + 0.5 * _swiglu(h, wg_ref, wu_ref, wd_ref)


def _ffn(x2d, g, wg, wu, wd):
    n_tok = x2d.shape[0]
    tm = TM_FFN
    tile = pl.BlockSpec((tm, D_MODEL), lambda i: (i, 0))
    return pl.pallas_call(
        _ffn_kernel,
        grid=(n_tok // tm,),
        in_specs=[tile, _const_spec(g.shape), _const_spec(wg.shape), _const_spec(wu.shape),
                  _const_spec(wd.shape)],
        out_specs=tile,
        out_shape=jax.ShapeDtypeStruct((n_tok, D_MODEL), F32),
        compiler_params=pltpu.CompilerParams(
            dimension_semantics=("parallel",), vmem_limit_bytes=_vmem_limit(48)),
        name="ffn",
    )(x2d, g, wg, wu, wd)


def _split3(c):
    hi = c.astype(BF16).astype(F32)
    r = c - hi
    mid = r.astype(BF16).astype(F32)
    lo = (r - mid).astype(BF16).astype(F32)
    return hi, mid, lo


def _attn_proj_kernel(x_ref, g_ref, wqkv_ref, bqkv_ref, wft_ref, bf_ref, qn_ref, kn_ref, sel_ref,
                      q_ref, kaug_ref, vt_ref, c_ref, c_carry):
    tm = x_ref.shape[0]

    @pl.when(pl.program_id(1) == 0)
    def _():
        c_carry[...] = jnp.zeros_like(c_carry)

    h = _rms_norm(x_ref[...], g_ref[...]).astype(BF16)
    qkv = jnp.dot(h, wqkv_ref[...], preferred_element_type=F32) + bqkv_ref[...]

    fl = lax.dot_general(wft_ref[...], h, (((1,), (1,)), ((), ())), preferred_element_type=F32)
    fl = fl[:N_HEADS, :] + bf_ref[...]
    log_f = jnp.minimum(fl, 0.0) - jnp.log1p(jnp.exp(-jnp.abs(fl)))

    lane = lax.broadcasted_iota(jnp.int32, log_f.shape, 1)
    cs = log_f
    d = 1
    while d < tm:
        cs = cs + jnp.where(lane >= d, pltpu.roll(cs, d, axis=1), 0.0)
        d *= 2
    c = cs + c_carry[:, 0:1]
    c_carry[...] = jnp.broadcast_to(c[:, tm - 1:tm], c_carry.shape)
    c_ref[...] = c

    hi, mid, lo = _split3(c * LOG2_E)
    ones = jnp.ones_like(c)
    pad = jnp.zeros((V7X_LANES - (N_SPLIT + 1) * N_HEADS, tm), F32)
    stack = jnp.concatenate([-hi, -mid, -lo, ones, pad], axis=0)
    extras = jnp.dot(stack.T.astype(BF16), sel_ref[...], preferred_element_type=F32)

    scale = HEAD_DIM ** -0.5 * LOG2_E
    for hh in range(N_HEADS):
        sl = slice(hh * HEAD_DIM, (hh + 1) * HEAD_DIM)
        qh = qkv[:, sl]
        qh = qh * lax.rsqrt(jnp.mean(qh * qh, axis=-1, keepdims=True) + NORM_EPS) * qn_ref[...]
        q_ref[:, sl] = (qh * scale).astype(BF16)
        kh = qkv[:, D_ATTN + hh * HEAD_DIM:D_ATTN + (hh + 1) * HEAD_DIM]
        kh = kh * lax.rsqrt(jnp.mean(kh * kh, axis=-1, keepdims=True) + NORM_EPS) * kn_ref[...]
        kaug_ref[hh, :, 0:HEAD_DIM] = kh.astype(BF16)
        kaug_ref[hh, :, HEAD_DIM:AUG_DIM] = extras[:, sl].astype(BF16)
        vh = qkv[:, 2 * D_ATTN + hh * HEAD_DIM:2 * D_ATTN + (hh + 1) * HEAD_DIM]
        vt_ref[hh, 0:HEAD_DIM, :] = vh.T.astype(BF16)
        vt_ref[hh, HEAD_DIM:VT_ROWS, :] = jnp.ones((VT_ROWS - HEAD_DIM, tm), BF16)


def _attn_proj(x, g, wqkv, bqkv, wft, bf, qn, kn, sel):
    bsz, seq, _ = x.shape
    tm = T_ATTN
    n_t = seq // tm
    out_shape = (
        jax.ShapeDtypeStruct((bsz, seq, D_ATTN), BF16),
        jax.ShapeDtypeStruct((bsz, N_HEADS, seq, AUG_DIM), BF16),
        jax.ShapeDtypeStruct((bsz, N_HEADS, n_t, VT_ROWS, tm), BF16),
        jax.ShapeDtypeStruct((bsz, N_HEADS, seq), F32),
    )
    out_specs = (
        pl.BlockSpec((None, tm, D_ATTN), lambda b, t: (b, t, 0)),
        pl.BlockSpec((None, N_HEADS, tm, AUG_DIM), lambda b, t: (b, 0, t, 0)),
        pl.BlockSpec((None, N_HEADS, None, VT_ROWS, tm), lambda b, t: (b, 0, t, 0, 0)),
        pl.BlockSpec((None, N_HEADS, tm), lambda b, t: (b, 0, t)),
    )
    consts = (g, wqkv, bqkv, wft, bf, qn, kn, sel)
    return pl.pallas_call(
        _attn_proj_kernel,
        grid=(bsz, n_t),
        in_specs=[pl.BlockSpec((None, tm, D_MODEL), lambda b, t: (b, t, 0))]
        + [_const_spec(a.shape) for a in consts],
        out_specs=out_specs,
        out_shape=out_shape,
        scratch_shapes=[pltpu.VMEM((N_HEADS, V7X_LANES), F32)],
        compiler_params=pltpu.CompilerParams(
            dimension_semantics=("arbitrary", "arbitrary"), vmem_limit_bytes=_vmem_limit(48)),
        name="attn_proj",
    )(x, *consts)


def _gelu_tanh(x):
    return 0.5 * x * (1.0 + jnp.tanh(0.7978845608028654 * (x + 0.044715 * (x * x * x))))


def _sigmoid(x):
    return 0.5 * jnp.tanh(0.5 * x) + 0.5


def _rnn_kernel(x_ref, g_ref, wr_ref, br_ref, cw_ref, cb_ref, wgate_ref, ba_ref, bx_ref, lam_ref,
                wo_ref, sga_ref, rnn_ref, tail, h_carry):
    tm = x_ref.shape[0]
    sub = V7X_SUBLANES
    gw = V7X_MXU_DIM

    @pl.when(pl.program_id(1) == 0)
    def _():
        tail[...] = jnp.zeros_like(tail)
        h_carry[...] = jnp.zeros_like(h_carry)

    h = _rms_norm(x_ref[...], g_ref[...]).astype(BF16)
    rest = jnp.dot(h, wr_ref[...], preferred_element_type=F32)

    def proj(lo, hi):
        return rest[:, lo:hi] + br_ref[:, lo:hi]

    sga_ref[...] = _sigmoid(proj(2 * D_RNN, 2 * D_RNN + D_MODEL))

    row = lax.broadcasted_iota(jnp.int32, (sub, gw), 0)
    y_rnn = None
    for j in range(D_RNN // gw):
        lo, hi = j * gw, (j + 1) * gw
        xr = proj(lo, hi)
        gr = proj(D_RNN + lo, D_RNN + hi)

        xe = jnp.concatenate([tail[:, lo:hi], xr], axis=0)
        conv = cb_ref[:, lo:hi]
        for tap in range(CONV_WIDTH - 1):
            shifted = pltpu.roll(xe, CONV_WIDTH - 1 - tap, axis=0)[sub:, :]
            conv = conv + shifted * cw_ref[tap:tap + 1, lo:hi]
        conv = conv + xr * cw_ref[CONV_WIDTH - 1:CONV_WIDTH, lo:hi]
        tail[:, lo:hi] = xr[tm - sub:, :]

        gz = jnp.dot(conv.astype(BF16), wgate_ref[j], preferred_element_type=F32)
        r = _sigmoid(gz[:, :gw] + ba_ref[:, lo:hi])
        i = _sigmoid(gz[:, gw:] + bx_ref[:, lo:hi])
        nlam = -lam_ref[:, lo:hi]
        softplus = jnp.maximum(nlam, 0.0) + jnp.log1p(jnp.exp(-jnp.abs(nlam)))
        log_a = (-LRU_C * softplus) * r
        a = jnp.exp(log_a)
        om = -jnp.tanh(log_a) * (a * a + 1.0)
        u = jnp.where(om == 0.0, 0.0, om * lax.rsqrt(om)) * (i * conv)

        h_prev = h_carry[:, lo:hi]
        hs = []
        for gi in range(tm // sub):
            aa = a[gi * sub:(gi + 1) * sub, :]
            bb = u[gi * sub:(gi + 1) * sub, :]
            d = 1
            while d < sub:
                keep = row >= d
                a_sh = jnp.where(keep, pltpu.roll(aa, d, axis=0), 1.0)
                b_sh = jnp.where(keep, pltpu.roll(bb, d, axis=0), 0.0)
                bb = aa * b_sh + bb
                aa = aa * a_sh
                d *= 2
            hg = aa * h_prev + bb
            hs.append(hg)
            h_prev = jnp.broadcast_to(hg[sub - 1:sub, :], (sub, gw))
        h_carry[:, lo:hi] = h_prev

        yr = (jnp.concatenate(hs, axis=0) * _gelu_tanh(gr)).astype(BF16)
        part = jnp.dot(yr, wo_ref[lo:hi, :], preferred_element_type=F32)
        y_rnn = part if y_rnn is None else y_rnn + part

    rnn_ref[...] = _sigmoid(proj(2 * D_RNN + D_MODEL, 2 * D_RNN + 2 * D_MODEL)) * y_rnn


def _rnn_branch(x, g, wr, br, cw, cb, wgate, ba, bx, lam, wo):
    bsz, seq, _ = x.shape
    tm = TM_RNN
    tile = pl.BlockSpec((None, tm, D_MODEL), lambda b, t: (b, t, 0))
    consts = (g, wr, br, cw, cb, wgate, ba, bx, lam, wo)
    out = jax.ShapeDtypeStruct((bsz, seq, D_MODEL), F32)
    return pl.pallas_call(
        _rnn_kernel,
        grid=(bsz, seq // tm),
        in_specs=[tile] + [_const_spec(a.shape) for a in consts],
        out_specs=(tile, tile),
        out_shape=(out, out),
        scratch_shapes=[
            pltpu.VMEM((V7X_SUBLANES, D_RNN), F32),
            pltpu.VMEM((V7X_SUBLANES, D_RNN), F32),
        ],
        compiler_params=pltpu.CompilerParams(
            dimension_semantics=("arbitrary", "arbitrary"), vmem_limit_bytes=_vmem_limit(48)),
        name="rnn_branch",
    )(x, *consts)


def _fox_attn_kernel(q_ref, c_ref, k_ref, vt_ref, o_ref, qt_aug, st0, st1, p0, p1, al0, al1,
                     bm0, bm1, m_s, acc):
    tq = q_ref.shape[0]
    tk = vt_ref.shape[2]
    hh = pl.program_id(1)
    qi = pl.program_id(2)
    st, pb, al, bm = (st0, st1), (p0, p1), (al0, al1), (bm0, bm1)

    qt_aug[0:HEAD_DIM, :] = q_ref[...].astype(F32).T.astype(BF16)
    hi, mid, lo = _split3(c_ref[pl.ds(hh, 1), :] * LOG2_E)
    row = lax.broadcasted_iota(jnp.int32, (V7X_SUBLANES, tq), 0)
    top = jnp.where(row == 0, hi, jnp.where(row == 1, mid, jnp.where(row == 2, lo,
          jnp.where(row < 2 * N_SPLIT, 1.0, 0.0))))
    qt_aug[HEAD_DIM:AUG_DIM, :] = jnp.concatenate(
        [top, jnp.zeros((V7X_LANES - V7X_SUBLANES, tq), F32)], axis=0).astype(BF16)

    m_s[...] = jnp.full(m_s.shape, MASK_VALUE, F32)
    acc[...] = jnp.zeros_like(acc)
    p1[...] = jnp.zeros_like(p1)
    al1[...] = jnp.ones_like(al1)

    def score(j, s, c0=0, with_max=True):
        kb = k_ref[pl.ds(pl.multiple_of(j * tk, tk), tk), :]
        blk = jnp.dot(kb, qt_aug[:, c0:], preferred_element_type=F32)
        st[s][:, c0:tq] = blk
        if with_max:
            bm[s][...] = jnp.max(blk, axis=0, keepdims=True)

    def softmax(s, c0=0, diagonal=False):
        st_ref = st[s]
        if diagonal:
            shape = (tk, tq - c0)
            visible = (lax.broadcasted_iota(jnp.int32, shape, 0)
                       <= lax.broadcasted_iota(jnp.int32, shape, 1))
            st_ref[:, c0:tq] = jnp.where(visible, st_ref[:, c0:tq], MASK_VALUE)
            blk_max = jnp.max(st_ref[:, c0:tq], axis=0, keepdims=True)
        else:
            blk_max = bm[s][...]
        m_old = m_s[:, c0:]
        m_new = jnp.maximum(m_old, blk_max)
        m_s[:, c0:] = m_new
        al[s][:, c0:] = jnp.exp2(m_old - m_new)
        pb[s][:, c0:tq] = jnp.exp2(st_ref[:, c0:tq] - m_new).astype(BF16)

    def pv(j, s, c0=0):
        acc[:, c0:tq] = al[s][:, c0:] * acc[:, c0:tq] + jnp.dot(
            vt_ref[j], pb[s][:, c0:tq], preferred_element_type=F32)

    score(0, 0)

    def trip(t, carry):
        a = KV_PER_Q * t
        for d in range(KV_PER_Q):
            s = d % 2
            score(a + d + 1, 1 - s)
            softmax(s)
            pv(jnp.maximum(a + d - 1, 0), 1 - s)
        return carry

    lax.fori_loop(0, qi, trip, 0)
    a = KV_PER_Q * qi
    for d in range(KV_PER_Q):
        s = d % 2
        if d + 1 < KV_PER_Q:
            score(a + d + 1, 1 - s, c0=(d + 1) * tk, with_max=False)
        softmax(s, c0=d * tk, diagonal=True)
        pv(jnp.maximum(a + d - 1, 0), 1 - s, c0=max(d - 1, 0) * tk)
    pv(a + KV_PER_Q - 1, (KV_PER_Q - 1) % 2, c0=(KV_PER_Q - 1) * tk)
    o_ref[...] = (acc[0:HEAD_DIM, 0:tq] / acc[HEAD_DIM:HEAD_DIM + 1, 0:tq]).T.astype(BF16)


def _fox_attn(q, c, kaug, vt):
    bsz, seq, _ = q.shape
    n_t, tk = vt.shape[2], vt.shape[4]
    tq = TQ_ATTN
    assert tq == KV_PER_Q * tk and KV_PER_Q % 2 == 0 and seq % tq == 0
    return pl.pallas_call(
        _fox_attn_kernel,
        grid=(bsz, N_HEADS, seq // tq),
        in_specs=[
            pl.BlockSpec((None, tq, HEAD_DIM), lambda b, h, i: (b, i, h)),
            pl.BlockSpec((None, N_HEADS, tq), lambda b, h, i: (b, 0, i)),
            pl.BlockSpec((None, None, seq, AUG_DIM), lambda b, h, i: (b, h, 0, 0)),
            pl.BlockSpec((None, None, n_t, VT_ROWS, tk), lambda b, h, i: (b, h, 0, 0, 0)),
        ],
        out_specs=pl.BlockSpec((None, tq, HEAD_DIM), lambda b, h, i: (b, i, h)),
        out_shape=jax.ShapeDtypeStruct((bsz, seq, D_ATTN), BF16),
        scratch_shapes=[
            pltpu.VMEM((AUG_DIM, tq), BF16),
            pltpu.VMEM((tk, tq + V7X_LANES), F32),
            pltpu.VMEM((tk, tq + V7X_LANES), F32),
            pltpu.VMEM((tk, tq + V7X_LANES), BF16),
            pltpu.VMEM((tk, tq + V7X_LANES), BF16),
            pltpu.VMEM((1, tq), F32),
            pltpu.VMEM((1, tq), F32),
            pltpu.VMEM((1, tq), F32),
            pltpu.VMEM((1, tq), F32),
            pltpu.VMEM((1, tq), F32),
            pltpu.VMEM((VT_ROWS, tq + V7X_LANES), F32),
        ],
        compiler_params=pltpu.CompilerParams(
            dimension_semantics=("arbitrary", "arbitrary", "arbitrary"),
            vmem_limit_bytes=_vmem_limit(56)),
        name="fox_attn",
    )(q, c, kaug, vt)


def _out_ffn_kernel(x_ref, attn_ref, sga_ref, rnn_ref, woa_ref, wout_ref, g_ref, wg_ref, wu_ref,
                    wd_ref, o_ref):
    y_attn = jnp.dot(attn_ref[...], woa_ref[...], preferred_element_type=F32)
    merged = sga_ref[...] * y_attn + rnn_ref[...]
    x1 = x_ref[...] + jnp.dot(merged.astype(BF16), wout_ref[...], preferred_element_type=F32)
    h = _rms_norm(x1, g_ref[...]).astype(BF16)
    o_ref[...] = x1 + 0.5 * _swiglu(h, wg_ref, wu_ref, wd_ref)


def _out_ffn(x2d, attn2d, sga2d, rnn2d, woa, wout, g, wg, wu, wd):
    n_tok = x2d.shape[0]
    tm = TM_OUT
    tile = pl.BlockSpec((tm, D_MODEL), lambda i: (i, 0))
    consts = (woa, wout, g, wg, wu, wd)
    return pl.pallas_call(
        _out_ffn_kernel,
        grid=(n_tok // tm,),
        in_specs=[tile, tile, tile, tile] + [_const_spec(a.shape) for a in consts],
        out_specs=tile,
        out_shape=jax.ShapeDtypeStruct((n_tok, D_MODEL), F32),
        compiler_params=pltpu.CompilerParams(
            dimension_semantics=("parallel",), vmem_limit_bytes=_vmem_limit(56)),
        name="out_ffn",
    )(x2d, attn2d, sga2d, rnn2d, *consts)


def _block_diag_tiles(w):
    per = V7X_MXU_DIM // RNN_BLOCK
    w4 = w.reshape(N_RNN_BLOCKS // per, per, RNN_BLOCK, RNN_BLOCK)
    eye = jnp.eye(per, dtype=w.dtype)
    return jnp.einsum("jarc,ab->jarbc", w4, eye).reshape(-1, V7X_MXU_DIM, V7X_MXU_DIM)


def _bias_selector():
    r = jnp.arange(V7X_LANES)[:, None]
    col = jnp.arange(N_HEADS * V7X_LANES)[None, :]
    head, pos = col // V7X_LANES, col % V7X_LANES
    ones_row = N_SPLIT * N_HEADS + head
    piece_row = (pos - N_SPLIT) * N_HEADS + head
    hit = jnp.where(pos < N_SPLIT, r == ones_row, (pos < 2 * N_SPLIT) & (r == piece_row))
    return hit.astype(BF16)


def kernel(x, ffn1_norm, ffn1_w_gate, ffn1_w_up, ffn1_w_down, mix_norm, w_in, b_in, q_norm, k_norm, conv_w, conv_b, lru_w_a, lru_b_a, lru_w_x, lru_b_x, lru_lambda, w_o_attn, w_o_rnn, w_out, ffn2_norm, ffn2_w_gate, ffn2_w_up, ffn2_w_down):
    bsz, seq, _ = x.shape
    depth = w_in.shape[0]
    n_tok = bsz * seq
    assert seq % T_ATTN == 0 and seq % TM_RNN == 0 and n_tok % TM_FFN == 0 and n_tok % TM_OUT == 0
    row = lambda v: v.reshape(1, -1).astype(F32)
    sel = _bias_selector()
    f_lo, f_hi = 3 * D_ATTN, 3 * D_ATTN + N_HEADS

    for l in range(depth):
        x2d = _ffn(x.reshape(n_tok, D_MODEL), row(ffn1_norm[l]), ffn1_w_gate[l].astype(BF16),
                   ffn1_w_up[l].astype(BF16), ffn1_w_down[l].astype(BF16))
        x = x2d.reshape(bsz, seq, D_MODEL)

        w_l, b_l = w_in[l], b_in[l]
        wft = jnp.zeros((2 * V7X_SUBLANES, D_MODEL), BF16).at[:N_HEADS].set(
            w_l[:, f_lo:f_hi].T.astype(BF16))
        q, kaug, vt, c = _attn_proj(
            x, row(mix_norm[l]), w_l[:, :f_lo].astype(BF16), row(b_l[:f_lo]), wft,
            b_l[f_lo:f_hi].reshape(N_HEADS, 1), row(q_norm[l]), row(k_norm[l]), sel)

        wgate = jnp.concatenate(
            [_block_diag_tiles(lru_w_a[l]), _block_diag_tiles(lru_w_x[l])], axis=2).astype(BF16)
        sga, rnn = _rnn_branch(
            x, row(mix_norm[l]), w_l[:, f_hi:].astype(BF16), row(b_l[f_hi:]), conv_w[l].astype(F32),
            row(conv_b[l]), wgate, row(lru_b_a[l]), row(lru_b_x[l]), row(lru_lambda[l]),
            w_o_rnn[l].astype(BF16))

        attn = _fox_attn(q, c, kaug, vt)

        x2d = _out_ffn(
            x.reshape(n_tok, D_MODEL), attn.reshape(n_tok, D_ATTN), sga.reshape(n_tok, D_MODEL),
            rnn.reshape(n_tok, D_MODEL), w_o_attn[l].astype(BF16), w_out[l].astype(BF16),
            row(ffn2_norm[l]), ffn2_w_gate[l].astype(BF16), ffn2_w_up[l].astype(BF16),
            ffn2_w_down[l].astype(BF16))
        x = x2d.reshape(bsz, seq, D_MODEL)
    return x
```

```python
import jax
import jax.numpy as jnp
from jax import lax
from jax.experimental import pallas as pl
from jax.experimental.pallas import tpu as pltpu

F32 = jnp.float32
BF16 = jnp.bfloat16

D_MODEL = 1024
N_HEADS = 8
HEAD_DIM = 128
D_ATTN = N_HEADS * HEAD_DIM
D_RNN = D_MODEL
N_RNN_BLOCKS = 16
RNN_BLOCK = D_RNN // N_RNN_BLOCKS
CONV_WIDTH = 4
LRU_C = 8.0
D_FF = 2816
NORM_EPS = 1e-6

V7X_LANES = 128
V7X_SUBLANES = 8
V7X_MXU_DIM = 256
V7X_VMEM_BYTES = 64 * 1024 * 1024

AUG_DIM = HEAD_DIM + V7X_LANES
N_SPLIT = 3

FF_CHUNKS = ((0, 1024), (1024, 2048), (2048, D_FF))

TM_FFN = 512
TM_OUT = 256
TM_RNN = 256
T_ATTN = 512
KV_PER_Q = 4
TQ_ATTN = KV_PER_Q * T_ATTN
SCRATCH_PAD = 2 * V7X_LANES
VT_ROWS = HEAD_DIM + 16

MASK_VALUE = -1e30
LOG2_E = 1.4426950408889634


def _vmem_limit(mib):
    return min(mib * 1024 * 1024, V7X_VMEM_BYTES - 4 * 1024 * 1024)


def _const_spec(shape):
    nd = len(shape)
    return pl.BlockSpec(shape, lambda *_: (0,) * nd, pipeline_mode=pl.Buffered(1))


def _rms_norm(x, g):
    return x * lax.rsqrt(jnp.mean(x * x, axis=-1, keepdims=True) + NORM_EPS) * g


def _swiglu(h_bf, wg_ref, wu_ref, wd_ref):
    acc = None
    for lo, hi in FF_CHUNKS:
        g = jnp.dot(h_bf, wg_ref[:, lo:hi], preferred_element_type=F32)
        u = jnp.dot(h_bf, wu_ref[:, lo:hi], preferred_element_type=F32)
        a = (g * jax.nn.sigmoid(g) * u).astype(BF16)
        part = jnp.dot(a, wd_ref[lo:hi, :], preferred_element_type=F32)
        acc = part if acc is None else acc + part
    return acc


def _ffn_kernel(x_ref, g_ref, wg_ref, wu_ref, wd_ref, o_ref):
    x = x_ref[...]
    h = _rms_norm(x, g_ref[...]).astype(BF16)
    o_ref[...] = x + 0.5 * _swiglu(h, wg_ref, wu_ref, wd_ref)


def _ffn(x2d, g, wg, wu, wd):
    n_tok = x2d.shape[0]
    tm = TM_FFN
    tile = pl.BlockSpec((tm, D_MODEL), lambda i: (i, 0))
    return pl.pallas_call(
        _ffn_kernel,
        grid=(n_tok // tm,),
        in_specs=[tile, _const_spec(g.shape), _const_spec(wg.shape), _const_spec(wu.shape),
                  _const_spec(wd.shape)],
        out_specs=tile,
        out_shape=jax.ShapeDtypeStruct((n_tok, D_MODEL), F32),
        compiler_params=pltpu.CompilerParams(
            dimension_semantics=("parallel",), vmem_limit_bytes=_vmem_limit(48)),
        name="ffn",
    )(x2d, g, wg, wu, wd)


def _split3(c):
    hi = c.astype(BF16).astype(F32)
    r = c - hi
    mid = r.astype(BF16).astype(F32)
    lo = (r - mid).astype(BF16).astype(F32)
    return hi, mid, lo


def _attn_proj_kernel(x_ref, g_ref, wqkv_ref, bqkv_ref, wft_ref, bf_ref, qn_ref, kn_ref, sel_ref,
                      q_ref, kaug_ref, vt_ref, c_ref, c_carry):
    tm = x_ref.shape[0]

    @pl.when(pl.program_id(1) == 0)
    def _():
        c_carry[...] = jnp.zeros_like(c_carry)

    h = _rms_norm(x_ref[...], g_ref[...]).astype(BF16)

    fl = lax.dot_general(wft_ref[...], h, (((1,), (1,)), ((), ())), preferred_element_type=F32)
    fl = fl[:N_HEADS, :] + bf_ref[...]
    log_f = jnp.minimum(fl, 0.0) - jnp.log1p(jnp.exp(-jnp.abs(fl)))

    lane = lax.broadcasted_iota(jnp.int32, log_f.shape, 1)
    cs = log_f
    d = 1
    while d < tm:
        cs = cs + jnp.where(lane >= d, pltpu.roll(cs, d, axis=1), 0.0)
        d *= 2
    c = cs + c_carry[:, 0:1]
    c_carry[...] = jnp.broadcast_to(c[:, tm - 1:tm], c_carry.shape)
    c_ref[...] = c

    hi, mid, lo = _split3(c * LOG2_E)
    ones = jnp.ones_like(c)
    pad = jnp.zeros((V7X_LANES - (N_SPLIT + 1) * N_HEADS, tm), F32)
    stack = jnp.concatenate([-hi, -mid, -lo, ones, pad], axis=0)
    stack_t = stack.T.astype(BF16)

    qkv = jnp.dot(h, wqkv_ref[...], preferred_element_type=F32) + bqkv_ref[...]
    extras = jnp.dot(stack_t, sel_ref[...], preferred_element_type=F32)

    scale = HEAD_DIM ** -0.5 * LOG2_E
    for hh in range(N_HEADS):
        sl = slice(hh * HEAD_DIM, (hh + 1) * HEAD_DIM)
        qh = qkv[:, sl]
        qh = qh * lax.rsqrt(jnp.mean(qh * qh, axis=-1, keepdims=True) + NORM_EPS) * qn_ref[...]
        q_ref[:, sl] = (qh * scale).astype(BF16)
        kh = qkv[:, D_ATTN + hh * HEAD_DIM:D_ATTN + (hh + 1) * HEAD_DIM]
        kh = kh * lax.rsqrt(jnp.mean(kh * kh, axis=-1, keepdims=True) + NORM_EPS) * kn_ref[...]
        kaug_ref[hh, :, 0:HEAD_DIM] = kh.astype(BF16)
        kaug_ref[hh, :, HEAD_DIM:AUG_DIM] = extras[:, sl].astype(BF16)
        vh = qkv[:, 2 * D_ATTN + hh * HEAD_DIM:2 * D_ATTN + (hh + 1) * HEAD_DIM]
        vt_ref[hh, 0:HEAD_DIM, :] = vh.T.astype(BF16)
        vt_ref[hh, HEAD_DIM:VT_ROWS, :] = jnp.ones((VT_ROWS - HEAD_DIM, tm), BF16)


def _attn_proj(x, g, wqkv, bqkv, wft, bf, qn, kn, sel):
    bsz, seq, _ = x.shape
    tm = T_ATTN
    n_t = seq // tm
    out_shape = (
        jax.ShapeDtypeStruct((bsz, seq, D_ATTN), BF16),
        jax.ShapeDtypeStruct((bsz, N_HEADS, seq, AUG_DIM), BF16),
        jax.ShapeDtypeStruct((bsz, N_HEADS, n_t, VT_ROWS, tm), BF16),
        jax.ShapeDtypeStruct((bsz, N_HEADS, seq), F32),
    )
    out_specs = (
        pl.BlockSpec((None, tm, D_ATTN), lambda b, t: (b, t, 0)),
        pl.BlockSpec((None, N_HEADS, tm, AUG_DIM), lambda b, t: (b, 0, t, 0)),
        pl.BlockSpec((None, N_HEADS, None, VT_ROWS, tm), lambda b, t: (b, 0, t, 0, 0)),
        pl.BlockSpec((None, N_HEADS, tm), lambda b, t: (b, 0, t)),
    )
    consts = (g, wqkv, bqkv, wft, bf, qn, kn, sel)
    return pl.pallas_call(
        _attn_proj_kernel,
        grid=(bsz, n_t),
        in_specs=[pl.BlockSpec((None, tm, D_MODEL), lambda b, t: (b, t, 0))]
        + [_const_spec(a.shape) for a in consts],
        out_specs=out_specs,
        out_shape=out_shape,
        scratch_shapes=[pltpu.VMEM((N_HEADS, V7X_LANES), F32)],
        compiler_params=pltpu.CompilerParams(
            dimension_semantics=("arbitrary", "arbitrary"), vmem_limit_bytes=_vmem_limit(48)),
        name="attn_proj",
    )(x, *consts)


def _gelu_tanh(x):
    return 0.5 * x * (1.0 + jnp.tanh(0.7978845608028654 * (x + 0.044715 * (x * x * x))))


def _sigmoid(x):
    return 0.5 * jnp.tanh(0.5 * x) + 0.5


def _rnn_kernel(x_ref, g_ref, perm_ref, unperm_ref, wxg_ref, bxg_ref, wm_ref, bm_ref, cw_ref,
                cb_ref, wgate_ref, ba_ref, bx_ref, lam_ref, wo_ref, sga_ref, rnn_ref, tail, h_carry):
    tm = x_ref.shape[0]
    sub = V7X_SUBLANES
    gw = V7X_MXU_DIM
    ngrp = D_RNN // gw
    nv = tm // sub
    ntail = CONV_WIDTH - 1
    mw = 2 * D_MODEL // ngrp

    @pl.when(pl.program_id(1) == 0)
    def _():
        tail[...] = jnp.zeros_like(tail)
        h_carry[...] = jnp.zeros_like(h_carry)

    sub_id = lax.broadcasted_iota(jnp.int32, (sub, gw), 0)
    first = sub_id == 0

    def vreg(arr, r):
        return arr[r * sub:(r + 1) * sub, :]

    h = _rms_norm(x_ref[...], g_ref[...]).astype(BF16)
    hp = jnp.dot(perm_ref[...], h, preferred_element_type=F32).astype(BF16)

    def xg_proj(j):
        return jnp.dot(hp, wxg_ref[j], preferred_element_type=F32) + bxg_ref[j]

    def merge_gate(j):
        lo, hi = j * mw, (j + 1) * mw
        return _sigmoid(jnp.dot(h, wm_ref[:, lo:hi], preferred_element_type=F32) + bm_ref[:, lo:hi])

    def conv_stage(j, xr):
        lo, hi = j * gw, (j + 1) * gw
        conv = cb_ref[:, lo:hi]
        for tap in range(CONV_WIDTH):
            k = CONV_WIDTH - 1 - tap
            if k == 0:
                shifted = xr
            else:
                head = []
                for i in range(k):
                    prev = pltpu.roll(tail[(ntail - k + i) * sub:(ntail - k + i + 1) * sub, lo:hi],
                                      1, axis=0)
                    cur = pltpu.roll(vreg(xr, nv - k + i), 1, axis=0)
                    head.append(jnp.where(first, prev, cur))
                shifted = jnp.concatenate(head + [xr[:(nv - k) * sub, :]], axis=0)
            conv = conv + shifted * cw_ref[tap:tap + 1, lo:hi]
        tail[:, lo:hi] = xr[(nv - ntail) * sub:, :]
        return conv

    def lru_stage(j, gz, conv, gr):
        lo, hi = j * gw, (j + 1) * gw
        r_gate = _sigmoid(gz[:, :gw] + ba_ref[:, lo:hi])
        i_gate = _sigmoid(gz[:, gw:] + bx_ref[:, lo:hi])
        nlam = -lam_ref[:, lo:hi]
        softplus = jnp.maximum(nlam, 0.0) + jnp.log1p(jnp.exp(-jnp.abs(nlam)))
        log_a = (-LRU_C * softplus) * r_gate
        a = jnp.exp(log_a)
        om = -jnp.tanh(log_a) * (a * a + 1.0)
        u = jnp.where(om == 0.0, 0.0, om * lax.rsqrt(om)) * (i_gate * conv)

        h_loc, a_cum = [vreg(u, 0)], [vreg(a, 0)]
        for r in range(1, nv):
            ar = vreg(a, r)
            h_loc.append(ar * h_loc[-1] + vreg(u, r))
            a_cum.append(ar * a_cum[-1])
        aa, bb = a_cum[-1], h_loc[-1]
        d = 1
        while d < sub:
            keep = sub_id >= d
            a_sh = jnp.where(keep, pltpu.roll(aa, d, axis=0), 1.0)
            b_sh = jnp.where(keep, pltpu.roll(bb, d, axis=0), 0.0)
            bb = aa * b_sh + bb
            aa = aa * a_sh
            d *= 2
        h_in = h_carry[:, lo:hi]
        chunk_end = aa * h_in + bb
        entry = jnp.where(first, h_in, pltpu.roll(chunk_end, 1, axis=0))
        h_carry[:, lo:hi] = jnp.broadcast_to(chunk_end[sub - 1:sub, :], (sub, gw))
        hs = jnp.concatenate([h_loc[r] + a_cum[r] * entry for r in range(nv)], axis=0)
        return (hs * _gelu_tanh(gr)).astype(BF16)

    def out_stage(j, yr):
        yr = jnp.dot(unperm_ref[...], yr, preferred_element_type=F32).astype(BF16)
        return jnp.dot(yr, wo_ref[j * gw:(j + 1) * gw, :], preferred_element_type=F32)

    rest = {0: xg_proj(0), 1: xg_proj(1)}
    y_rnn, yr_prev, g_rnn = None, None, []
    for j in range(ngrp):
        conv = conv_stage(j, rest[j][:, :gw])
        gz = jnp.dot(conv.astype(BF16), wgate_ref[j], preferred_element_type=F32)
        mg = merge_gate(j)
        if (j + 1) * mw <= D_MODEL:
            sga_ref[:, j * mw:(j + 1) * mw] = mg
        else:
            g_rnn.append(mg)
        if j + 2 < ngrp:
            rest[j + 2] = xg_proj(j + 2)
        if yr_prev is not None:
            part = out_stage(j - 1, yr_prev)
            y_rnn = part if y_rnn is None else y_rnn + part
        yr_prev = lru_stage(j, gz, conv, rest[j][:, gw:])
    y_rnn = y_rnn + out_stage(ngrp - 1, yr_prev)
    rnn_ref[...] = jnp.concatenate(g_rnn, axis=1) * y_rnn


def _time_permutation(tm):
    rho = jnp.arange(tm)
    src = (tm // V7X_SUBLANES) * (rho % V7X_SUBLANES) + rho // V7X_SUBLANES
    return (src[:, None] == jnp.arange(tm)[None, :]).astype(BF16)


def _rnn_branch(x, g, wr, br, cw, cb, wgate, ba, bx, lam, wo):
    bsz, seq, _ = x.shape
    tm = TM_RNN
    tile = pl.BlockSpec((None, tm, D_MODEL), lambda b, t: (b, t, 0))
    perm = _time_permutation(tm)
    ngrp = D_RNN // V7X_MXU_DIM
    wxg = wr[:, :2 * D_RNN].reshape(D_MODEL, 2, ngrp, V7X_MXU_DIM).transpose(2, 0, 1, 3).reshape(
        ngrp, D_MODEL, 2 * V7X_MXU_DIM)
    bxg = br[:, :2 * D_RNN].reshape(1, 2, ngrp, V7X_MXU_DIM).transpose(2, 0, 1, 3).reshape(
        ngrp, 1, 2 * V7X_MXU_DIM)
    consts = (g, perm, perm.T, wxg, bxg, wr[:, 2 * D_RNN:], br[:, 2 * D_RNN:], cw, cb, wgate, ba,
              bx, lam, wo)
    out = jax.ShapeDtypeStruct((bsz, seq, D_MODEL), F32)
    return pl.pallas_call(
        _rnn_kernel,
        grid=(bsz, seq // tm),
        in_specs=[tile] + [_const_spec(a.shape) for a in consts],
        out_specs=(tile, tile),
        out_shape=(out, out),
        scratch_shapes=[
            pltpu.VMEM(((CONV_WIDTH - 1) * V7X_SUBLANES, D_RNN), F32),
            pltpu.VMEM((V7X_SUBLANES, D_RNN), F32),
        ],
        compiler_params=pltpu.CompilerParams(
            dimension_semantics=("arbitrary", "arbitrary"), vmem_limit_bytes=_vmem_limit(48)),
        name="rnn_branch",
    )(x, *consts)


def _fox_attn_kernel(q_ref, c_ref, k_ref, vt_ref, o_ref, qt_aug, st0, st1, p0, p1, al0, al1,
                     bm0, bm1, m_s, acc):
    tq = q_ref.shape[0]
    tk = vt_ref.shape[2]
    hh = pl.program_id(1)
    qi = pl.program_id(2)
    st, pb, al, bm = (st0, st1), (p0, p1), (al0, al1), (bm0, bm1)

    qt_aug[0:HEAD_DIM, :] = q_ref[...].astype(F32).T.astype(BF16)
    hi, mid, lo = _split3(c_ref[pl.ds(hh, 1), :] * LOG2_E)
    row = lax.broadcasted_iota(jnp.int32, (V7X_SUBLANES, tq), 0)
    top = jnp.where(row == 0, hi, jnp.where(row == 1, mid, jnp.where(row == 2, lo,
          jnp.where(row < 2 * N_SPLIT, 1.0, 0.0))))
    qt_aug[HEAD_DIM:AUG_DIM, :] = jnp.concatenate(
        [top, jnp.zeros((V7X_LANES - V7X_SUBLANES, tq), F32)], axis=0).astype(BF16)

    m_s[...] = jnp.full(m_s.shape, MASK_VALUE, F32)
    acc[...] = jnp.zeros_like(acc)
    p1[...] = jnp.zeros_like(p1)
    al1[...] = jnp.ones_like(al1)

    def score(j, s, c0=0, with_max=True):
        kb = k_ref[pl.ds(pl.multiple_of(j * tk, tk), tk), :]
        blk = jnp.dot(kb, qt_aug[:, c0:], preferred_element_type=F32)
        st[s][:, c0:tq] = blk
        if with_max:
            bm[s][...] = jnp.max(blk, axis=0, keepdims=True)

    def softmax(s, c0=0, diagonal=False):
        st_ref = st[s]
        if diagonal:
            shape = (tk, tq - c0)
            visible = (lax.broadcasted_iota(jnp.int32, shape, 0)
                       <= lax.broadcasted_iota(jnp.int32, shape, 1))
            st_ref[:, c0:tq] = jnp.where(visible, st_ref[:, c0:tq], MASK_VALUE)
            blk_max = jnp.max(st_ref[:, c0:tq], axis=0, keepdims=True)
        else:
            blk_max = bm[s][...]
        m_old = m_s[:, c0:]
        m_new = jnp.maximum(m_old, blk_max)
        m_s[:, c0:] = m_new
        al[s][:, c0:] = jnp.exp2(m_old - m_new)
        pb[s][:, c0:tq] = jnp.exp2(st_ref[:, c0:tq] - m_new).astype(BF16)

    def pv(j, s, c0=0):
        acc[:, c0:tq] = al[s][:, c0:] * acc[:, c0:tq] + jnp.dot(
            vt_ref[j], pb[s][:, c0:tq], preferred_element_type=F32)

    def trip(t, carry):
        a = KV_PER_Q * t
        score(a, 0)
        score(a + 1, 1)
        for d in range(KV_PER_Q):
            s = d % 2
            softmax(s)
            pv(jnp.maximum(a + d - 1, 0), 1 - s)
            if d + 2 < KV_PER_Q:
                score(a + d + 2, s)
        return carry

    lax.fori_loop(0, qi, trip, 0)
    a = KV_PER_Q * qi
    score(a, 0, with_max=False)
    score(a + 1, 1, c0=tk, with_max=False)
    for d in range(KV_PER_Q):
        s = d % 2
        softmax(s, c0=d * tk, diagonal=True)
        pv(jnp.maximum(a + d - 1, 0), 1 - s, c0=max(d - 1, 0) * tk)
        if d + 2 < KV_PER_Q:
            score(a + d + 2, s, c0=(d + 2) * tk, with_max=False)
    pv(a + KV_PER_Q - 1, (KV_PER_Q - 1) % 2, c0=(KV_PER_Q - 1) * tk)
    o_ref[...] = (acc[0:HEAD_DIM, 0:tq] / acc[HEAD_DIM:HEAD_DIM + 1, 0:tq]).T.astype(BF16)


def _fox_attn(q, c, kaug, vt):
    bsz, seq, _ = q.shape
    n_t, tk = vt.shape[2], vt.shape[4]
    tq = TQ_ATTN
    assert tq == KV_PER_Q * tk and KV_PER_Q % 2 == 0 and seq % tq == 0
    return pl.pallas_call(
        _fox_attn_kernel,
        grid=(bsz, N_HEADS, seq // tq),
        in_specs=[
            pl.BlockSpec((None, tq, HEAD_DIM), lambda b, h, i: (b, i, h)),
            pl.BlockSpec((None, N_HEADS, tq), lambda b, h, i: (b, 0, i)),
            pl.BlockSpec((None, None, seq, AUG_DIM), lambda b, h, i: (b, h, 0, 0)),
            pl.BlockSpec((None, None, n_t, VT_ROWS, tk), lambda b, h, i: (b, h, 0, 0, 0)),
        ],
        out_specs=pl.BlockSpec((None, tq, HEAD_DIM), lambda b, h, i: (b, i, h)),
        out_shape=jax.ShapeDtypeStruct((bsz, seq, D_ATTN), BF16),
        scratch_shapes=[
            pltpu.VMEM((AUG_DIM, tq), BF16),
            pltpu.VMEM((tk, tq + SCRATCH_PAD), F32),
            pltpu.VMEM((tk, tq + SCRATCH_PAD), F32),
            pltpu.VMEM((tk, tq + SCRATCH_PAD), BF16),
            pltpu.VMEM((tk, tq + SCRATCH_PAD), BF16),
            pltpu.VMEM((1, tq), F32),
            pltpu.VMEM((1, tq), F32),
            pltpu.VMEM((1, tq), F32),
            pltpu.VMEM((1, tq), F32),
            pltpu.VMEM((1, tq), F32),
            pltpu.VMEM((VT_ROWS, tq + SCRATCH_PAD), F32),
        ],
        compiler_params=pltpu.CompilerParams(
            dimension_semantics=("arbitrary", "arbitrary", "arbitrary"),
            vmem_limit_bytes=_vmem_limit(56)),
        name="fox_attn",
    )(q, c, kaug, vt)


def _out_ffn_kernel(x_ref, attn_ref, sga_ref, rnn_ref, woa_ref, wout_ref, g_ref, wg_ref, wu_ref,
                    wd_ref, o_ref):
    y_attn = jnp.dot(attn_ref[...], woa_ref[...], preferred_element_type=F32)
    merged = sga_ref[...] * y_attn + rnn_ref[...]
    x1 = x_ref[...] + jnp.dot(merged.astype(BF16), wout_ref[...], preferred_element_type=F32)
    h = _rms_norm(x1, g_ref[...]).astype(BF16)
    o_ref[...] = x1 + 0.5 * _swiglu(h, wg_ref, wu_ref, wd_ref)


def _out_ffn(x2d, attn2d, sga2d, rnn2d, woa, wout, g, wg, wu, wd):
    n_tok = x2d.shape[0]
    tm = TM_OUT
    tile = pl.BlockSpec((tm, D_MODEL), lambda i: (i, 0))
    consts = (woa, wout, g, wg, wu, wd)
    return pl.pallas_call(
        _out_ffn_kernel,
        grid=(n_tok // tm,),
        in_specs=[tile, tile, tile, tile] + [_const_spec(a.shape) for a in consts],
        out_specs=tile,
        out_shape=jax.ShapeDtypeStruct((n_tok, D_MODEL), F32),
        compiler_params=pltpu.CompilerParams(
            dimension_semantics=("parallel",), vmem_limit_bytes=_vmem_limit(56)),
        name="out_ffn",
    )(x2d, attn2d, sga2d, rnn2d, *consts)


def _block_diag_tiles(w):
    per = V7X_MXU_DIM // RNN_BLOCK
    w4 = w.reshape(N_RNN_BLOCKS // per, per, RNN_BLOCK, RNN_BLOCK)
    eye = jnp.eye(per, dtype=w.dtype)
    return jnp.einsum("jarc,ab->jarbc", w4, eye).reshape(-1, V7X_MXU_DIM, V7X_MXU_DIM)


def _bias_selector():
    r = jnp.arange(V7X_LANES)[:, None]
    col = jnp.arange(N_HEADS * V7X_LANES)[None, :]
    head, pos = col // V7X_LANES, col % V7X_LANES
    ones_row = N_SPLIT * N_HEADS + head
    piece_row = (pos - N_SPLIT) * N_HEADS + head
    hit = jnp.where(pos < N_SPLIT, r == ones_row, (pos < 2 * N_SPLIT) & (r == piece_row))
    return hit.astype(BF16)


def kernel(x, ffn1_norm, ffn1_w_gate, ffn1_w_up, ffn1_w_down, mix_norm, w_in, b_in, q_norm, k_norm, conv_w, conv_b, lru_w_a, lru_b_a, lru_w_x, lru_b_x, lru_lambda, w_o_attn, w_o_rnn, w_out, ffn2_norm, ffn2_w_gate, ffn2_w_up, ffn2_w_down):
    bsz, seq, _ = x.shape
    depth = w_in.shape[0]
    n_tok = bsz * seq
    assert seq % T_ATTN == 0 and seq % TM_RNN == 0 and n_tok % TM_FFN == 0 and n_tok % TM_OUT == 0
    row = lambda v: v.reshape(1, -1).astype(F32)
    sel = _bias_selector()
    f_lo, f_hi = 3 * D_ATTN, 3 * D_ATTN + N_HEADS

    for l in range(depth):
        x2d = _ffn(x.reshape(n_tok, D_MODEL), row(ffn1_norm[l]), ffn1_w_gate[l].astype(BF16),
                   ffn1_w_up[l].astype(BF16), ffn1_w_down[l].astype(BF16))
        x = x2d.reshape(bsz, seq, D_MODEL)

        w_l, b_l = w_in[l], b_in[l]
        wft = jnp.zeros((2 * V7X_SUBLANES, D_MODEL), BF16).at[:N_HEADS].set(
            w_l[:, f_lo:f_hi].T.astype(BF16))
        q, kaug, vt, c = _attn_proj(
            x, row(mix_norm[l]), w_l[:, :f_lo].astype(BF16), row(b_l[:f_lo]), wft,
            b_l[f_lo:f_hi].reshape(N_HEADS, 1), row(q_norm[l]), row(k_norm[l]), sel)

        wgate = jnp.concatenate(
            [_block_diag_tiles(lru_w_a[l]), _block_diag_tiles(lru_w_x[l])], axis=2).astype(BF16)
        sga, rnn = _rnn_branch(
            x, row(mix_norm[l]), w_l[:, f_hi:].astype(BF16), row(b_l[f_hi:]), conv_w[l].astype(F32),
            row(conv_b[l]), wgate, row(lru_b_a[l]), row(lru_b_x[l]), row(lru_lambda[l]),
            w_o_rnn[l].astype(BF16))

        attn = _fox_attn(q, c, kaug, vt)

        x2d = _out_ffn(
            x.reshape(n_tok, D_MODEL), attn.reshape(n_tok, D_ATTN), sga.reshape(n_tok, D_MODEL),
            rnn.reshape(n_tok, D_MODEL), w_o_attn[l].astype(BF16), w_out[l].astype(BF16),
            row(ffn2_norm[l]), ffn2_w_gate[l].astype(BF16), ffn2_w_up[l].astype(BF16),
            ffn2_w_down[l].astype(BF16))
        x = x2d.reshape(bsz, seq, D_MODEL)
    return x
```

```python
import jax
import jax.numpy as jnp
from jax import lax
from jax.experimental import pallas as pl
from jax.experimental.pallas import tpu as pltpu

F32 = jnp.float32
BF16 = jnp.bfloat16

D_MODEL = 1024
N_HEADS = 8
HEAD_DIM = 128
D_ATTN = N_HEADS * HEAD_DIM
D_RNN = D_MODEL
N_RNN_BLOCKS = 16
RNN_BLOCK = D_RNN // N_RNN_BLOCKS
CONV_WIDTH = 4
LRU_C = 8.0
D_FF = 2816
NORM_EPS = 1e-6

V7X_LANES = 128
V7X_SUBLANES = 8
V7X_MXU_DIM = 256
V7X_VMEM_BYTES = 64 * 1024 * 1024

AUG_DIM = HEAD_DIM + V7X_LANES
N_SPLIT = 3

FF_CHUNKS = ((0, 1024), (1024, 2048), (2048, D_FF))

TM_FFN = 1024
TM_OUT = 512
TM_RNN = 256
T_ATTN = 512
KV_PER_Q = 4
TQ_ATTN = KV_PER_Q * T_ATTN
SCRATCH_PAD = 2 * V7X_LANES
VT_ROWS = HEAD_DIM + 16

MASK_VALUE = -1e30
LOG2_E = 1.4426950408889634


def _vmem_limit(mib):
    return min(mib * 1024 * 1024, V7X_VMEM_BYTES - 4 * 1024 * 1024)


def _const_spec(shape):
    nd = len(shape)
    return pl.BlockSpec(shape, lambda *_: (0,) * nd, pipeline_mode=pl.Buffered(1))


def _rms_norm(x, g):
    return x * lax.rsqrt(jnp.mean(x * x, axis=-1, keepdims=True) + NORM_EPS) * g


def _swiglu(h_bf, wg_ref, wu_ref, wd_ref):
    acc = None
    for lo, hi in FF_CHUNKS:
        g = jnp.dot(h_bf, wg_ref[:, lo:hi], preferred_element_type=F32)
        u = jnp.dot(h_bf, wu_ref[:, lo:hi], preferred_element_type=F32)
        a = (g * jax.nn.sigmoid(g) * u).astype(BF16)
        part = jnp.dot(a, wd_ref[lo:hi, :], preferred_element_type=F32)
        acc = part if acc is None else acc + part
    return acc


def _ffn_kernel(x_ref, g_ref, wg_ref, wu_ref, wd_ref, o_ref):
    x = x_ref[...]
    h = _rms_norm(x, g_ref[...]).astype(BF16)
    o_ref[...] = x + 0.5 * _swiglu(h, wg_ref, wu_ref, wd_ref)


def _ffn(x2d, g, wg, wu, wd):
    n_tok = x2d.shape[0]
    tm = TM_FFN
    tile = pl.BlockSpec((tm, D_MODEL), lambda i: (i, 0))
    return pl.pallas_call(
        _ffn_kernel,
        grid=(n_tok // tm,),
        in_specs=[tile, _const_spec(g.shape), _const_spec(wg.shape), _const_spec(wu.shape),
                  _const_spec(wd.shape)],
        out_specs=tile,
        out_shape=jax.ShapeDtypeStruct((n_tok, D_MODEL), F32),
        compiler_params=pltpu.CompilerParams(
            dimension_semantics=("parallel",), vmem_limit_bytes=_vmem_limit(48)),
        name="ffn",
    )(x2d, g, wg, wu, wd)


def _split3(c):
    hi = c.astype(BF16).astype(F32)
    r = c - hi
    mid = r.astype(BF16).astype(F32)
    lo = (r - mid).astype(BF16).astype(F32)
    return hi, mid, lo


def _attn_proj_kernel(x_ref, g_ref, wqkv_ref, bqkv_ref, wft_ref, bf_ref, qn_ref, kn_ref, sel_ref,
                      qt_ref, kaug_ref, vt_ref, c_ref, c_carry):
    tm = x_ref.shape[0]

    @pl.when(pl.program_id(1) == 0)
    def _():
        c_carry[...] = jnp.zeros_like(c_carry)

    h = _rms_norm(x_ref[...], g_ref[...]).astype(BF16)

    fl = lax.dot_general(wft_ref[...], h, (((1,), (1,)), ((), ())), preferred_element_type=F32)
    fl = fl[:N_HEADS, :] + bf_ref[...]
    log_f = jnp.minimum(fl, 0.0) - jnp.log1p(jnp.exp(-jnp.abs(fl)))

    lane = lax.broadcasted_iota(jnp.int32, log_f.shape, 1)
    cs = log_f
    d = 1
    while d < tm:
        cs = cs + jnp.where(lane >= d, pltpu.roll(cs, d, axis=1), 0.0)
        d *= 2
    c = cs + c_carry[:, 0:1]
    c_carry[...] = jnp.broadcast_to(c[:, tm - 1:tm], c_carry.shape)
    c_ref[...] = c

    hi, mid, lo = _split3(c * LOG2_E)
    ones = jnp.ones_like(c)
    pad = jnp.zeros((V7X_LANES - (N_SPLIT + 1) * N_HEADS, tm), F32)
    stack = jnp.concatenate([-hi, -mid, -lo, ones, pad], axis=0)
    stack_t = stack.T.astype(BF16)

    qkv = jnp.dot(h, wqkv_ref[...], preferred_element_type=F32) + bqkv_ref[...]
    extras = jnp.dot(stack_t, sel_ref[...], preferred_element_type=F32)

    scale = HEAD_DIM ** -0.5 * LOG2_E
    for hh in range(N_HEADS):
        sl = slice(hh * HEAD_DIM, (hh + 1) * HEAD_DIM)
        qh = qkv[:, sl]
        qh = qh * lax.rsqrt(jnp.mean(qh * qh, axis=-1, keepdims=True) + NORM_EPS) * qn_ref[...]
        qt_ref[hh] = (qh * scale).T.astype(BF16)
        kh = qkv[:, D_ATTN + hh * HEAD_DIM:D_ATTN + (hh + 1) * HEAD_DIM]
        kh = kh * lax.rsqrt(jnp.mean(kh * kh, axis=-1, keepdims=True) + NORM_EPS) * kn_ref[...]
        kaug_ref[hh, :, 0:HEAD_DIM] = kh.astype(BF16)
        kaug_ref[hh, :, HEAD_DIM:AUG_DIM] = extras[:, sl].astype(BF16)
        vh = qkv[:, 2 * D_ATTN + hh * HEAD_DIM:2 * D_ATTN + (hh + 1) * HEAD_DIM]
        vt_ref[hh, 0:HEAD_DIM, :] = vh.T.astype(BF16)
        vt_ref[hh, HEAD_DIM:VT_ROWS, :] = jnp.ones((VT_ROWS - HEAD_DIM, tm), BF16)


def _attn_proj(x, g, wqkv, bqkv, wft, bf, qn, kn, sel):
    bsz, seq, _ = x.shape
    tm = T_ATTN
    n_t = seq // tm
    out_shape = (
        jax.ShapeDtypeStruct((bsz, N_HEADS, n_t, HEAD_DIM, tm), BF16),
        jax.ShapeDtypeStruct((bsz, N_HEADS, seq, AUG_DIM), BF16),
        jax.ShapeDtypeStruct((bsz, N_HEADS, n_t, VT_ROWS, tm), BF16),
        jax.ShapeDtypeStruct((bsz, N_HEADS, seq), F32),
    )
    out_specs = (
        pl.BlockSpec((None, N_HEADS, None, HEAD_DIM, tm), lambda b, t: (b, 0, t, 0, 0)),
        pl.BlockSpec((None, N_HEADS, tm, AUG_DIM), lambda b, t: (b, 0, t, 0)),
        pl.BlockSpec((None, N_HEADS, None, VT_ROWS, tm), lambda b, t: (b, 0, t, 0, 0)),
        pl.BlockSpec((None, N_HEADS, tm), lambda b, t: (b, 0, t)),
    )
    consts = (g, wqkv, bqkv, wft, bf, qn, kn, sel)
    return pl.pallas_call(
        _attn_proj_kernel,
        grid=(bsz, n_t),
        in_specs=[pl.BlockSpec((None, tm, D_MODEL), lambda b, t: (b, t, 0))]
        + [_const_spec(a.shape) for a in consts],
        out_specs=out_specs,
        out_shape=out_shape,
        scratch_shapes=[pltpu.VMEM((N_HEADS, V7X_LANES), F32)],
        compiler_params=pltpu.CompilerParams(
            dimension_semantics=("arbitrary", "arbitrary"), vmem_limit_bytes=_vmem_limit(48)),
        name="attn_proj",
    )(x, *consts)


def _gelu_tanh(x):
    return 0.5 * x * (1.0 + jnp.tanh(0.7978845608028654 * (x + 0.044715 * (x * x * x))))


def _sigmoid(x):
    return 0.5 * jnp.tanh(0.5 * x) + 0.5


def _rnn_kernel(x_ref, g_ref, perm_ref, unperm_ref, wxg_ref, bxg_ref, wm_ref, bm_ref, cw_ref,
                cb_ref, wgate_ref, ba_ref, bx_ref, lam_ref, wo_ref, sga_ref, rnn_ref, tail, h_carry):
    tm = x_ref.shape[0]
    sub = V7X_SUBLANES
    gw = V7X_MXU_DIM
    ngrp = D_RNN // gw
    nv = tm // sub
    ntail = CONV_WIDTH - 1
    mw = 2 * D_MODEL // ngrp

    @pl.when(pl.program_id(1) == 0)
    def _():
        tail[...] = jnp.zeros_like(tail)
        h_carry[...] = jnp.zeros_like(h_carry)

    sub_id = lax.broadcasted_iota(jnp.int32, (sub, gw), 0)
    first = sub_id == 0

    def vreg(arr, r):
        return arr[r * sub:(r + 1) * sub, :]

    h = _rms_norm(x_ref[...], g_ref[...]).astype(BF16)
    hp = jnp.dot(perm_ref[...], h, preferred_element_type=F32).astype(BF16)

    def xg_proj(j):
        return jnp.dot(hp, wxg_ref[j], preferred_element_type=F32) + bxg_ref[j]

    def merge_gate(j):
        lo, hi = j * mw, (j + 1) * mw
        return _sigmoid(jnp.dot(h, wm_ref[:, lo:hi], preferred_element_type=F32) + bm_ref[:, lo:hi])

    def conv_stage(j, xr):
        lo, hi = j * gw, (j + 1) * gw
        conv = cb_ref[:, lo:hi]
        for tap in range(CONV_WIDTH):
            k = CONV_WIDTH - 1 - tap
            if k == 0:
                shifted = xr
            else:
                head = []
                for i in range(k):
                    prev = pltpu.roll(tail[(ntail - k + i) * sub:(ntail - k + i + 1) * sub, lo:hi],
                                      1, axis=0)
                    cur = pltpu.roll(vreg(xr, nv - k + i), 1, axis=0)
                    head.append(jnp.where(first, prev, cur))
                shifted = jnp.concatenate(head + [xr[:(nv - k) * sub, :]], axis=0)
            conv = conv + shifted * cw_ref[tap:tap + 1, lo:hi]
        tail[:, lo:hi] = xr[(nv - ntail) * sub:, :]
        return conv

    def lru_stage(j, gz, conv, gr):
        lo, hi = j * gw, (j + 1) * gw
        r_gate = _sigmoid(gz[:, :gw] + ba_ref[:, lo:hi])
        i_gate = _sigmoid(gz[:, gw:] + bx_ref[:, lo:hi])
        nlam = -lam_ref[:, lo:hi]
        softplus = jnp.maximum(nlam, 0.0) + jnp.log1p(jnp.exp(-jnp.abs(nlam)))
        log_a = (-LRU_C * softplus) * r_gate
        a = jnp.exp(log_a)
        om = -jnp.tanh(log_a) * (a * a + 1.0)
        u = jnp.where(om == 0.0, 0.0, om * lax.rsqrt(om)) * (i_gate * conv)

        h_loc, a_cum = [vreg(u, 0)], [vreg(a, 0)]
        for r in range(1, nv):
            ar = vreg(a, r)
            h_loc.append(ar * h_loc[-1] + vreg(u, r))
            a_cum.append(ar * a_cum[-1])
        aa, bb = a_cum[-1], h_loc[-1]
        d = 1
        while d < sub:
            keep = sub_id >= d
            a_sh = jnp.where(keep, pltpu.roll(aa, d, axis=0), 1.0)
            b_sh = jnp.where(keep, pltpu.roll(bb, d, axis=0), 0.0)
            bb = aa * b_sh + bb
            aa = aa * a_sh
            d *= 2
        h_in = h_carry[:, lo:hi]
        chunk_end = aa * h_in + bb
        entry = jnp.where(first, h_in, pltpu.roll(chunk_end, 1, axis=0))
        h_carry[:, lo:hi] = jnp.broadcast_to(chunk_end[sub - 1:sub, :], (sub, gw))
        hs = jnp.concatenate([h_loc[r] + a_cum[r] * entry for r in range(nv)], axis=0)
        return (hs * _gelu_tanh(gr)).astype(BF16)

    def out_stage(j, yr):
        yr = jnp.dot(unperm_ref[...], yr, preferred_element_type=F32).astype(BF16)
        return jnp.dot(yr, wo_ref[j * gw:(j + 1) * gw, :], preferred_element_type=F32)

    def gate_stage(j, conv):
        return jnp.dot(conv.astype(BF16), wgate_ref[j], preferred_element_type=F32)

    mg = {0: merge_gate(0)}
    rest = {0: xg_proj(0), 1: xg_proj(1)}
    conv = {0: conv_stage(0, rest[0][:, :gw])}
    gz = {0: gate_stage(0, conv[0])}
    mg[1] = merge_gate(1)
    rest[2] = xg_proj(2)
    conv[1] = conv_stage(1, rest[1][:, :gw])
    yr = {0: lru_stage(0, gz[0], conv[0], rest[0][:, gw:])}
    gz[1] = gate_stage(1, conv[1])
    y_rnn = out_stage(0, yr[0])
    rest[3] = xg_proj(3)
    conv[2] = conv_stage(2, rest[2][:, :gw])
    yr[1] = lru_stage(1, gz[1], conv[1], rest[1][:, gw:])
    gz[2] = gate_stage(2, conv[2])
    y_rnn = y_rnn + out_stage(1, yr[1])
    conv[3] = conv_stage(3, rest[3][:, :gw])
    yr[2] = lru_stage(2, gz[2], conv[2], rest[2][:, gw:])
    gz[3] = gate_stage(3, conv[3])
    mg[2] = merge_gate(2)
    y_rnn = y_rnn + out_stage(2, yr[2])
    yr[3] = lru_stage(3, gz[3], conv[3], rest[3][:, gw:])
    mg[3] = merge_gate(3)
    y_rnn = y_rnn + out_stage(3, yr[3])

    sga_ref[...] = jnp.concatenate([mg[0], mg[1]], axis=1)
    rnn_ref[...] = jnp.concatenate([mg[2], mg[3]], axis=1) * y_rnn


def _time_permutation(tm):
    rho = jnp.arange(tm)
    src = (tm // V7X_SUBLANES) * (rho % V7X_SUBLANES) + rho // V7X_SUBLANES
    return (src[:, None] == jnp.arange(tm)[None, :]).astype(BF16)


def _rnn_branch(x, g, wr, br, cw, cb, wgate, ba, bx, lam, wo):
    bsz, seq, _ = x.shape
    tm = TM_RNN
    tile = pl.BlockSpec((None, tm, D_MODEL), lambda b, t: (b, t, 0))
    perm = _time_permutation(tm)
    ngrp = D_RNN // V7X_MXU_DIM
    wxg = wr[:, :2 * D_RNN].reshape(D_MODEL, 2, ngrp, V7X_MXU_DIM).transpose(2, 0, 1, 3).reshape(
        ngrp, D_MODEL, 2 * V7X_MXU_DIM)
    bxg = br[:, :2 * D_RNN].reshape(1, 2, ngrp, V7X_MXU_DIM).transpose(2, 0, 1, 3).reshape(
        ngrp, 1, 2 * V7X_MXU_DIM)
    consts = (g, perm, perm.T, wxg, bxg, wr[:, 2 * D_RNN:], br[:, 2 * D_RNN:], cw, cb, wgate, ba,
              bx, lam, wo)
    out = jax.ShapeDtypeStruct((bsz, seq, D_MODEL), F32)
    return pl.pallas_call(
        _rnn_kernel,
        grid=(bsz, seq // tm),
        in_specs=[tile] + [_const_spec(a.shape) for a in consts],
        out_specs=(tile, tile),
        out_shape=(out, out),
        scratch_shapes=[
            pltpu.VMEM(((CONV_WIDTH - 1) * V7X_SUBLANES, D_RNN), F32),
            pltpu.VMEM((V7X_SUBLANES, D_RNN), F32),
        ],
        compiler_params=pltpu.CompilerParams(
            dimension_semantics=("arbitrary", "arbitrary"), vmem_limit_bytes=_vmem_limit(48)),
        name="rnn_branch",
    )(x, *consts)


def _fox_attn_kernel(qt_ref, c_ref, k_ref, vt_ref, o_ref, qt_aug, st0, st1, p0, p1, al0, al1,
                     bm0, bm1, m_s, acc):
    tq = o_ref.shape[0]
    tk = vt_ref.shape[2]
    hh = pl.program_id(1)
    qi = pl.program_id(2)
    st, pb, al, bm = (st0, st1), (p0, p1), (al0, al1), (bm0, bm1)

    for ci in range(qt_ref.shape[0]):
        qt_aug[0:HEAD_DIM, ci * tk:(ci + 1) * tk] = qt_ref[ci]
    hi, mid, lo = _split3(c_ref[pl.ds(hh, 1), :] * LOG2_E)
    row = lax.broadcasted_iota(jnp.int32, (V7X_SUBLANES, tq), 0)
    top = jnp.where(row == 0, hi, jnp.where(row == 1, mid, jnp.where(row == 2, lo,
          jnp.where(row < 2 * N_SPLIT, 1.0, 0.0))))
    qt_aug[HEAD_DIM:AUG_DIM, :] = jnp.concatenate(
        [top, jnp.zeros((V7X_LANES - V7X_SUBLANES, tq), F32)], axis=0).astype(BF16)

    m_s[...] = jnp.full(m_s.shape, MASK_VALUE, F32)
    acc[...] = jnp.zeros_like(acc)
    p1[...] = jnp.zeros_like(p1)
    al1[...] = jnp.ones_like(al1)

    def score(j, s, c0=0, with_max=True):
        kb = k_ref[pl.ds(pl.multiple_of(j * tk, tk), tk), :]
        blk = jnp.dot(kb, qt_aug[:, c0:], preferred_element_type=F32)
        st[s][:, c0:tq] = blk
        if with_max:
            bm[s][...] = jnp.max(blk, axis=0, keepdims=True)

    def softmax(s, c0=0, diagonal=False):
        st_ref = st[s]
        if diagonal:
            shape = (tk, tq - c0)
            visible = (lax.broadcasted_iota(jnp.int32, shape, 0)
                       <= lax.broadcasted_iota(jnp.int32, shape, 1))
            st_ref[:, c0:tq] = jnp.where(visible, st_ref[:, c0:tq], MASK_VALUE)
            blk_max = jnp.max(st_ref[:, c0:tq], axis=0, keepdims=True)
        else:
            blk_max = bm[s][...]
        m_old = m_s[:, c0:]
        m_new = jnp.maximum(m_old, blk_max)
        m_s[:, c0:] = m_new
        al[s][:, c0:] = jnp.exp2(m_old - m_new)
        pb[s][:, c0:tq] = jnp.exp2(st_ref[:, c0:tq] - m_new).astype(BF16)

    def pv(j, s, c0=0):
        acc[:, c0:tq] = al[s][:, c0:] * acc[:, c0:tq] + jnp.dot(
            vt_ref[j], pb[s][:, c0:tq], preferred_element_type=F32)

    def trip(t, carry):
        a = KV_PER_Q * t
        score(a, 0)
        score(a + 1, 1)
        for d in range(KV_PER_Q):
            s = d % 2
            softmax(s)
            pv(jnp.maximum(a + d - 1, 0), 1 - s)
            if d + 2 < KV_PER_Q:
                score(a + d + 2, s)
        return carry

    lax.fori_loop(0, qi, trip, 0)
    a = KV_PER_Q * qi
    score(a, 0, with_max=False)
    score(a + 1, 1, c0=tk, with_max=False)
    for d in range(KV_PER_Q):
        s = d % 2
        softmax(s, c0=d * tk, diagonal=True)
        pv(jnp.maximum(a + d - 1, 0), 1 - s, c0=max(d - 1, 0) * tk)
        if d + 2 < KV_PER_Q:
            score(a + d + 2, s, c0=(d + 2) * tk, with_max=False)
    pv(a + KV_PER_Q - 1, (KV_PER_Q - 1) % 2, c0=(KV_PER_Q - 1) * tk)
    o_ref[...] = (acc[0:HEAD_DIM, 0:tq] / acc[HEAD_DIM:HEAD_DIM + 1, 0:tq]).T.astype(BF16)


def _fox_attn(qt, c, kaug, vt):
    bsz, seq = c.shape[0], c.shape[2]
    n_t, tk = vt.shape[2], vt.shape[4]
    tq = TQ_ATTN
    assert tq == KV_PER_Q * tk and KV_PER_Q % 2 == 0 and seq % tq == 0
    return pl.pallas_call(
        _fox_attn_kernel,
        grid=(bsz, N_HEADS, seq // tq),
        in_specs=[
            pl.BlockSpec((None, None, KV_PER_Q, HEAD_DIM, tk), lambda b, h, i: (b, h, i, 0, 0)),
            pl.BlockSpec((None, N_HEADS, tq), lambda b, h, i: (b, 0, i)),
            pl.BlockSpec((None, None, seq, AUG_DIM), lambda b, h, i: (b, h, 0, 0)),
            pl.BlockSpec((None, None, n_t, VT_ROWS, tk), lambda b, h, i: (b, h, 0, 0, 0)),
        ],
        out_specs=pl.BlockSpec((None, tq, HEAD_DIM), lambda b, h, i: (b, i, h)),
        out_shape=jax.ShapeDtypeStruct((bsz, seq, D_ATTN), BF16),
        scratch_shapes=[
            pltpu.VMEM((AUG_DIM, tq), BF16),
            pltpu.VMEM((tk, tq + SCRATCH_PAD), F32),
            pltpu.VMEM((tk, tq + SCRATCH_PAD), F32),
            pltpu.VMEM((tk, tq + SCRATCH_PAD), BF16),
            pltpu.VMEM((tk, tq + SCRATCH_PAD), BF16),
            pltpu.VMEM((1, tq), F32),
            pltpu.VMEM((1, tq), F32),
            pltpu.VMEM((1, tq), F32),
            pltpu.VMEM((1, tq), F32),
            pltpu.VMEM((1, tq), F32),
            pltpu.VMEM((VT_ROWS, tq + SCRATCH_PAD), F32),
        ],
        compiler_params=pltpu.CompilerParams(
            dimension_semantics=("arbitrary", "arbitrary", "arbitrary"),
            vmem_limit_bytes=_vmem_limit(56)),
        name="fox_attn",
    )(qt, c, kaug, vt)


def _out_ffn_kernel(x_ref, attn_ref, sga_ref, rnn_ref, woa_ref, wout_ref, g_ref, wg_ref, wu_ref,
                    wd_ref, o_ref):
    y_attn = jnp.dot(attn_ref[...], woa_ref[...], preferred_element_type=F32)
    merged = sga_ref[...] * y_attn + rnn_ref[...]
    x1 = x_ref[...] + jnp.dot(merged.astype(BF16), wout_ref[...], preferred_element_type=F32)
    h = _rms_norm(x1, g_ref[...]).astype(BF16)
    o_ref[...] = x1 + 0.5 * _swiglu(h, wg_ref, wu_ref, wd_ref)


def _out_ffn(x2d, attn2d, sga2d, rnn2d, woa, wout, g, wg, wu, wd):
    n_tok = x2d.shape[0]
    tm = TM_OUT
    tile = pl.BlockSpec((tm, D_MODEL), lambda i: (i, 0))
    consts = (woa, wout, g, wg, wu, wd)
    return pl.pallas_call(
        _out_ffn_kernel,
        grid=(n_tok // tm,),
        in_specs=[tile, tile, tile, tile] + [_const_spec(a.shape) for a in consts],
        out_specs=tile,
        out_shape=jax.ShapeDtypeStruct((n_tok, D_MODEL), F32),
        compiler_params=pltpu.CompilerParams(
            dimension_semantics=("parallel",), vmem_limit_bytes=_vmem_limit(56)),
        name="out_ffn",
    )(x2d, attn2d, sga2d, rnn2d, *consts)


def _block_diag_tiles(w):
    per = V7X_MXU_DIM // RNN_BLOCK
    w4 = w.reshape(N_RNN_BLOCKS // per, per, RNN_BLOCK, RNN_BLOCK)
    eye = jnp.eye(per, dtype=w.dtype)
    return jnp.einsum("jarc,ab->jarbc", w4, eye).reshape(-1, V7X_MXU_DIM, V7X_MXU_DIM)


def _bias_selector():
    r = jnp.arange(V7X_LANES)[:, None]
    col = jnp.arange(N_HEADS * V7X_LANES)[None, :]
    head, pos = col // V7X_LANES, col % V7X_LANES
    ones_row = N_SPLIT * N_HEADS + head
    piece_row = (pos - N_SPLIT) * N_HEADS + head
    hit = jnp.where(pos < N_SPLIT, r == ones_row, (pos < 2 * N_SPLIT) & (r == piece_row))
    return hit.astype(BF16)


def kernel(x, ffn1_norm, ffn1_w_gate, ffn1_w_up, ffn1_w_down, mix_norm, w_in, b_in, q_norm, k_norm, conv_w, conv_b, lru_w_a, lru_b_a, lru_w_x, lru_b_x, lru_lambda, w_o_attn, w_o_rnn, w_out, ffn2_norm, ffn2_w_gate, ffn2_w_up, ffn2_w_down):
    bsz, seq, _ = x.shape
    depth = w_in.shape[0]
    n_tok = bsz * seq
    assert seq % T_ATTN == 0 and seq % TM_RNN == 0 and n_tok % TM_FFN == 0 and n_tok % TM_OUT == 0
    row = lambda v: v.reshape(1, -1).astype(F32)
    sel = _bias_selector()
    f_lo, f_hi = 3 * D_ATTN, 3 * D_ATTN + N_HEADS

    for l in range(depth):
        x2d = _ffn(x.reshape(n_tok, D_MODEL), row(ffn1_norm[l]), ffn1_w_gate[l].astype(BF16),
                   ffn1_w_up[l].astype(BF16), ffn1_w_down[l].astype(BF16))
        x = x2d.reshape(bsz, seq, D_MODEL)

        w_l, b_l = w_in[l], b_in[l]
        wft = jnp.zeros((2 * V7X_SUBLANES, D_MODEL), BF16).at[:N_HEADS].set(
            w_l[:, f_lo:f_hi].T.astype(BF16))
        qt, kaug, vt, c = _attn_proj(
            x, row(mix_norm[l]), w_l[:, :f_lo].astype(BF16), row(b_l[:f_lo]), wft,
            b_l[f_lo:f_hi].reshape(N_HEADS, 1), row(q_norm[l]), row(k_norm[l]), sel)

        wgate = jnp.concatenate(
            [_block_diag_tiles(lru_w_a[l]), _block_diag_tiles(lru_w_x[l])], axis=2).astype(BF16)
        sga, rnn = _rnn_branch(
            x, row(mix_norm[l]), w_l[:, f_hi:].astype(BF16), row(b_l[f_hi:]), conv_w[l].astype(F32),
            row(conv_b[l]), wgate, row(lru_b_a[l]), row(lru_b_x[l]), row(lru_lambda[l]),
            w_o_rnn[l].astype(BF16))

        attn = _fox_attn(qt, c, kaug, vt)

        x2d = _out_ffn(
            x.reshape(n_tok, D_MODEL), attn.reshape(n_tok, D_ATTN), sga.reshape(n_tok, D_MODEL),
            rnn.reshape(n_tok, D_MODEL), w_o_attn[l].astype(BF16), w_out[l].astype(BF16),
            row(ffn2_norm[l]), ffn2_w_gate[l].astype(BF16), ffn2_w_up[l].astype(BF16),
            ffn2_w_down[l].astype(BF16))
        x = x2d.reshape(bsz, seq, D_MODEL)
    return x
```

```python
import jax
import jax.numpy as jnp
from jax import lax
from jax.experimental import pallas as pl
from jax.experimental.pallas import tpu as pltpu

F32 = jnp.float32
BF16 = jnp.bfloat16

D_MODEL = 1024
N_HEADS = 8
HEAD_DIM = 128
D_ATTN = N_HEADS * HEAD_DIM
D_RNN = D_MODEL
N_RNN_BLOCKS = 16
RNN_BLOCK = D_RNN // N_RNN_BLOCKS
CONV_WIDTH = 4
LRU_C = 8.0
D_FF = 2816
NORM_EPS = 1e-6

V7X_LANES = 128
V7X_SUBLANES = 8
V7X_MXU_DIM = 256
V7X_VMEM_BYTES = 64 * 1024 * 1024

AUG_DIM = HEAD_DIM + V7X_LANES
N_SPLIT = 3

FF_CHUNKS = ((0, 1024), (1024, 2048), (2048, D_FF))

TM_FFN = 1024
TM_OUT = 512
TM_RNN = 512
T_ATTN = 512
KV_PER_Q = 4
TQ_ATTN = KV_PER_Q * T_ATTN
SCRATCH_PAD = 2 * V7X_LANES
VT_ROWS = HEAD_DIM + 16

MASK_VALUE = -1e30
LOG2_E = 1.4426950408889634


def _vmem_limit(mib):
    return min(mib * 1024 * 1024, V7X_VMEM_BYTES - 4 * 1024 * 1024)


def _param_spec(arr, layer=None):
    if layer is None:
        block, idx = arr.shape, (0,) * arr.ndim
    else:
        block, idx = (None,) + arr.shape[1:], (layer,) + (0,) * (arr.ndim - 1)
    return pl.BlockSpec(block, lambda *_: idx, pipeline_mode=pl.Buffered(1))


def _rms_norm(x, g):
    return x * lax.rsqrt(jnp.mean(x * x, axis=-1, keepdims=True) + NORM_EPS) * g


def _swiglu(h_bf, wg_ref, wu_ref, wd_ref):
    acc = None
    for lo, hi in FF_CHUNKS:
        g = jnp.dot(h_bf, wg_ref[:, lo:hi], preferred_element_type=F32)
        u = jnp.dot(h_bf, wu_ref[:, lo:hi], preferred_element_type=F32)
        a = (g * jax.nn.sigmoid(g) * u).astype(BF16)
        part = jnp.dot(a, wd_ref[lo:hi, :], preferred_element_type=F32)
        acc = part if acc is None else acc + part
    return acc


def _ffn_kernel(x_ref, g_ref, wg_ref, wu_ref, wd_ref, o_ref):
    x = x_ref[...]
    h = _rms_norm(x, g_ref[...]).astype(BF16)
    o_ref[...] = x + 0.5 * _swiglu(h, wg_ref, wu_ref, wd_ref)


def _ffn(x2d, layer, g, wg, wu, wd):
    n_tok = x2d.shape[0]
    tm = TM_FFN
    tile = pl.BlockSpec((tm, D_MODEL), lambda i: (i, 0))
    return pl.pallas_call(
        _ffn_kernel,
        grid=(n_tok // tm,),
        in_specs=[tile] + [_param_spec(a, layer) for a in (g, wg, wu, wd)],
        out_specs=tile,
        out_shape=jax.ShapeDtypeStruct((n_tok, D_MODEL), F32),
        compiler_params=pltpu.CompilerParams(
            dimension_semantics=("parallel",), vmem_limit_bytes=_vmem_limit(48)),
        name="ffn",
    )(x2d, g, wg, wu, wd)


def _split3(c):
    hi = c.astype(BF16).astype(F32)
    r = c - hi
    mid = r.astype(BF16).astype(F32)
    lo = (r - mid).astype(BF16).astype(F32)
    return hi, mid, lo


def _attn_proj_kernel(x_ref, g_ref, wqkv_ref, bqkv_ref, wft_ref, bf_ref, qn_ref, kn_ref, sel_ref,
                      qt_ref, kaug_ref, vt_ref, c_ref, c_carry):
    tm = x_ref.shape[0]

    @pl.when(pl.program_id(1) == 0)
    def _():
        c_carry[...] = jnp.zeros_like(c_carry)

    h = _rms_norm(x_ref[...], g_ref[...]).astype(BF16)

    fl = lax.dot_general(wft_ref[...], h, (((1,), (1,)), ((), ())), preferred_element_type=F32)
    fl = fl[:N_HEADS, :] + bf_ref[...]
    log_f = jnp.minimum(fl, 0.0) - jnp.log1p(jnp.exp(-jnp.abs(fl)))

    lane = lax.broadcasted_iota(jnp.int32, log_f.shape, 1)
    cs = log_f
    d = 1
    while d < tm:
        cs = cs + jnp.where(lane >= d, pltpu.roll(cs, d, axis=1), 0.0)
        d *= 2
    c = cs + c_carry[:, 0:1]
    c_carry[...] = jnp.broadcast_to(c[:, tm - 1:tm], c_carry.shape)
    c_ref[...] = c

    hi, mid, lo = _split3(c * LOG2_E)
    ones = jnp.ones_like(c)
    pad = jnp.zeros((V7X_LANES - (N_SPLIT + 1) * N_HEADS, tm), F32)
    stack = jnp.concatenate([-hi, -mid, -lo, ones, pad], axis=0)
    stack_t = stack.T.astype(BF16)

    qkv = jnp.dot(h, wqkv_ref[...], preferred_element_type=F32) + bqkv_ref[...]
    extras = jnp.dot(stack_t, sel_ref[...], preferred_element_type=F32)

    scale = HEAD_DIM ** -0.5 * LOG2_E
    for hh in range(N_HEADS):
        sl = slice(hh * HEAD_DIM, (hh + 1) * HEAD_DIM)
        qh = qkv[:, sl]
        qh = qh * lax.rsqrt(jnp.mean(qh * qh, axis=-1, keepdims=True) + NORM_EPS) * qn_ref[...]
        qt_ref[hh] = (qh * scale).T.astype(BF16)
        kh = qkv[:, D_ATTN + hh * HEAD_DIM:D_ATTN + (hh + 1) * HEAD_DIM]
        kh = kh * lax.rsqrt(jnp.mean(kh * kh, axis=-1, keepdims=True) + NORM_EPS) * kn_ref[...]
        kaug_ref[hh, :, 0:HEAD_DIM] = kh.astype(BF16)
        kaug_ref[hh, :, HEAD_DIM:AUG_DIM] = extras[:, sl].astype(BF16)
        vh = qkv[:, 2 * D_ATTN + hh * HEAD_DIM:2 * D_ATTN + (hh + 1) * HEAD_DIM]
        vt_ref[hh, 0:HEAD_DIM, :] = vh.T.astype(BF16)
        vt_ref[hh, HEAD_DIM:VT_ROWS, :] = jnp.ones((VT_ROWS - HEAD_DIM, tm), BF16)


def _attn_proj(x, layer, g, wqkv, bqkv, wft, bf, qn, kn, sel):
    bsz, seq, _ = x.shape
    tm = T_ATTN
    n_t = seq // tm
    out_shape = (
        jax.ShapeDtypeStruct((bsz, N_HEADS, n_t, HEAD_DIM, tm), BF16),
        jax.ShapeDtypeStruct((bsz, N_HEADS, seq, AUG_DIM), BF16),
        jax.ShapeDtypeStruct((bsz, N_HEADS, n_t, VT_ROWS, tm), BF16),
        jax.ShapeDtypeStruct((bsz, N_HEADS, seq), F32),
    )
    out_specs = (
        pl.BlockSpec((None, N_HEADS, None, HEAD_DIM, tm), lambda b, t: (b, 0, t, 0, 0)),
        pl.BlockSpec((None, N_HEADS, tm, AUG_DIM), lambda b, t: (b, 0, t, 0)),
        pl.BlockSpec((None, N_HEADS, None, VT_ROWS, tm), lambda b, t: (b, 0, t, 0, 0)),
        pl.BlockSpec((None, N_HEADS, tm), lambda b, t: (b, 0, t)),
    )
    stacked = (g, wqkv, bqkv, wft, bf, qn, kn)
    consts = stacked + (sel,)
    return pl.pallas_call(
        _attn_proj_kernel,
        grid=(bsz, n_t),
        in_specs=[pl.BlockSpec((None, tm, D_MODEL), lambda b, t: (b, t, 0))]
        + [_param_spec(a, layer) for a in stacked] + [_param_spec(sel)],
        out_specs=out_specs,
        out_shape=out_shape,
        scratch_shapes=[pltpu.VMEM((N_HEADS, V7X_LANES), F32)],
        compiler_params=pltpu.CompilerParams(
            dimension_semantics=("arbitrary", "arbitrary"), vmem_limit_bytes=_vmem_limit(48)),
        name="attn_proj",
    )(x, *consts)


def _gelu_tanh(x):
    return 0.5 * x * (1.0 + jnp.tanh(0.7978845608028654 * (x + 0.044715 * (x * x * x))))


def _sigmoid(x):
    return 0.5 * jnp.tanh(0.5 * x) + 0.5


def _rnn_kernel(x_ref, g_ref, perm_ref, unperm_ref, wxg_ref, bxg_ref, wm_ref, bm_ref, cw_ref,
                cb_ref, wgate_ref, ba_ref, bx_ref, lam_ref, wo_ref, sga_ref, rnn_ref, tail, h_carry):
    tm = x_ref.shape[0]
    sub = V7X_SUBLANES
    gw = V7X_MXU_DIM
    ngrp = D_RNN // gw
    nv = tm // sub
    ntail = CONV_WIDTH - 1
    mw = 2 * D_MODEL // ngrp

    @pl.when(pl.program_id(1) == 0)
    def _():
        tail[...] = jnp.zeros_like(tail)
        h_carry[...] = jnp.zeros_like(h_carry)

    sub_id = lax.broadcasted_iota(jnp.int32, (sub, gw), 0)
    first = sub_id == 0

    def vreg(arr, r):
        return arr[r * sub:(r + 1) * sub, :]

    h = _rms_norm(x_ref[...], g_ref[...]).astype(BF16)
    hp = jnp.dot(perm_ref[...], h, preferred_element_type=F32).astype(BF16)

    def xg_proj(j):
        return jnp.dot(hp, wxg_ref[j], preferred_element_type=F32) + bxg_ref[j]

    def merge_gate(j):
        lo, hi = j * mw, (j + 1) * mw
        return _sigmoid(jnp.dot(h, wm_ref[:, lo:hi], preferred_element_type=F32) + bm_ref[:, lo:hi])

    def conv_stage(j, xr):
        lo, hi = j * gw, (j + 1) * gw
        conv = cb_ref[:, lo:hi]
        for tap in range(CONV_WIDTH):
            k = CONV_WIDTH - 1 - tap
            if k == 0:
                shifted = xr
            else:
                head = []
                for i in range(k):
                    prev = pltpu.roll(tail[(ntail - k + i) * sub:(ntail - k + i + 1) * sub, lo:hi],
                                      1, axis=0)
                    cur = pltpu.roll(vreg(xr, nv - k + i), 1, axis=0)
                    head.append(jnp.where(first, prev, cur))
                shifted = jnp.concatenate(head + [xr[:(nv - k) * sub, :]], axis=0)
            conv = conv + shifted * cw_ref[tap:tap + 1, lo:hi]
        tail[:, lo:hi] = xr[(nv - ntail) * sub:, :]
        return conv

    def lru_stage(j, gz, conv, gr):
        lo, hi = j * gw, (j + 1) * gw
        r_gate = _sigmoid(gz[:, :gw] + ba_ref[:, lo:hi])
        i_gate = _sigmoid(gz[:, gw:] + bx_ref[:, lo:hi])
        nlam = -lam_ref[:, lo:hi]
        softplus = jnp.maximum(nlam, 0.0) + jnp.log1p(jnp.exp(-jnp.abs(nlam)))
        log_a = (-LRU_C * softplus) * r_gate
        a = jnp.exp(log_a)
        om = -jnp.tanh(log_a) * (a * a + 1.0)
        u = jnp.where(om == 0.0, 0.0, om * lax.rsqrt(om)) * (i_gate * conv)

        h_loc, a_cum = [vreg(u, 0)], [vreg(a, 0)]
        for r in range(1, nv):
            ar = vreg(a, r)
            h_loc.append(ar * h_loc[-1] + vreg(u, r))
            a_cum.append(ar * a_cum[-1])
        aa, bb = a_cum[-1], h_loc[-1]
        d = 1
        while d < sub:
            keep = sub_id >= d
            a_sh = jnp.where(keep, pltpu.roll(aa, d, axis=0), 1.0)
            b_sh = jnp.where(keep, pltpu.roll(bb, d, axis=0), 0.0)
            bb = aa * b_sh + bb
            aa = aa * a_sh
            d *= 2
        h_in = h_carry[:, lo:hi]
        chunk_end = aa * h_in + bb
        entry = jnp.where(first, h_in, pltpu.roll(chunk_end, 1, axis=0))
        h_carry[:, lo:hi] = jnp.broadcast_to(chunk_end[sub - 1:sub, :], (sub, gw))
        hs = jnp.concatenate([h_loc[r] + a_cum[r] * entry for r in range(nv)], axis=0)
        return (hs * _gelu_tanh(gr)).astype(BF16)

    def out_stage(j, yr):
        yr = jnp.dot(unperm_ref[...], yr, preferred_element_type=F32).astype(BF16)
        return jnp.dot(yr, wo_ref[j * gw:(j + 1) * gw, :], preferred_element_type=F32)

    def gate_stage(j, conv):
        return jnp.dot(conv.astype(BF16), wgate_ref[j], preferred_element_type=F32)

    mg = {0: merge_gate(0)}
    rest = {0: xg_proj(0), 1: xg_proj(1)}
    conv = {0: conv_stage(0, rest[0][:, :gw])}
    gz = {0: gate_stage(0, conv[0])}
    mg[1] = merge_gate(1)
    rest[2] = xg_proj(2)
    conv[1] = conv_stage(1, rest[1][:, :gw])
    yr = {0: lru_stage(0, gz[0], conv[0], rest[0][:, gw:])}
    gz[1] = gate_stage(1, conv[1])
    y_rnn = out_stage(0, yr[0])
    rest[3] = xg_proj(3)
    conv[2] = conv_stage(2, rest[2][:, :gw])
    yr[1] = lru_stage(1, gz[1], conv[1], rest[1][:, gw:])
    gz[2] = gate_stage(2, conv[2])
    y_rnn = y_rnn + out_stage(1, yr[1])
    conv[3] = conv_stage(3, rest[3][:, :gw])
    yr[2] = lru_stage(2, gz[2], conv[2], rest[2][:, gw:])
    gz[3] = gate_stage(3, conv[3])
    mg[2] = merge_gate(2)
    y_rnn = y_rnn + out_stage(2, yr[2])
    yr[3] = lru_stage(3, gz[3], conv[3], rest[3][:, gw:])
    mg[3] = merge_gate(3)
    y_rnn = y_rnn + out_stage(3, yr[3])

    sga_ref[...] = jnp.concatenate([mg[0], mg[1]], axis=1)
    rnn_ref[...] = jnp.concatenate([mg[2], mg[3]], axis=1) * y_rnn


def _time_permutation(tm):
    rho = jnp.arange(tm)
    src = (tm // V7X_SUBLANES) * (rho % V7X_SUBLANES) + rho // V7X_SUBLANES
    return (src[:, None] == jnp.arange(tm)[None, :]).astype(BF16)


def _rnn_branch(x, layer, g, wxg, bxg, wm, bm, cw, cb, wgate, ba, bx, lam, wo):
    bsz, seq, _ = x.shape
    tm = TM_RNN
    tile = pl.BlockSpec((None, tm, D_MODEL), lambda b, t: (b, t, 0))
    perm = _time_permutation(tm)
    shared = (perm, perm.T)
    stacked = (wxg, bxg, wm, bm, cw, cb, wgate, ba, bx, lam, wo)
    consts = (g,) + shared + stacked
    out = jax.ShapeDtypeStruct((bsz, seq, D_MODEL), F32)
    return pl.pallas_call(
        _rnn_kernel,
        grid=(bsz, seq // tm),
        in_specs=[tile, _param_spec(g, layer)] + [_param_spec(a) for a in shared]
        + [_param_spec(a, layer) for a in stacked],
        out_specs=(tile, tile),
        out_shape=(out, out),
        scratch_shapes=[
            pltpu.VMEM(((CONV_WIDTH - 1) * V7X_SUBLANES, D_RNN), F32),
            pltpu.VMEM((V7X_SUBLANES, D_RNN), F32),
        ],
        compiler_params=pltpu.CompilerParams(
            dimension_semantics=("arbitrary", "arbitrary"), vmem_limit_bytes=_vmem_limit(48)),
        name="rnn_branch",
    )(x, *consts)


def _fox_attn_kernel(qt_ref, c_ref, k_ref, vt_ref, o_ref, qt_aug, st0, st1, p0, p1, al0, al1,
                     bm0, bm1, m_s, acc):
    tq = o_ref.shape[0]
    tk = vt_ref.shape[2]
    hh = pl.program_id(1)
    qi = pl.program_id(2)
    st, pb, al, bm = (st0, st1), (p0, p1), (al0, al1), (bm0, bm1)

    for ci in range(qt_ref.shape[0]):
        qt_aug[0:HEAD_DIM, ci * tk:(ci + 1) * tk] = qt_ref[ci]
    hi, mid, lo = _split3(c_ref[pl.ds(hh, 1), :] * LOG2_E)
    row = lax.broadcasted_iota(jnp.int32, (V7X_SUBLANES, tq), 0)
    top = jnp.where(row == 0, hi, jnp.where(row == 1, mid, jnp.where(row == 2, lo,
          jnp.where(row < 2 * N_SPLIT, 1.0, 0.0))))
    qt_aug[HEAD_DIM:AUG_DIM, :] = jnp.concatenate(
        [top, jnp.zeros((V7X_LANES - V7X_SUBLANES, tq), F32)], axis=0).astype(BF16)

    m_s[...] = jnp.full(m_s.shape, MASK_VALUE, F32)
    acc[...] = jnp.zeros_like(acc)
    p1[...] = jnp.zeros_like(p1)
    al1[...] = jnp.ones_like(al1)

    def score(j, s, c0=0, with_max=True):
        kb = k_ref[pl.ds(pl.multiple_of(j * tk, tk), tk), :]
        blk = jnp.dot(kb, qt_aug[:, c0:], preferred_element_type=F32)
        st[s][:, c0:tq] = blk
        if with_max:
            bm[s][...] = jnp.max(blk, axis=0, keepdims=True)

    def softmax(s, c0=0, diagonal=False):
        st_ref = st[s]
        if diagonal:
            shape = (tk, tq - c0)
            visible = (lax.broadcasted_iota(jnp.int32, shape, 0)
                       <= lax.broadcasted_iota(jnp.int32, shape, 1))
            st_ref[:, c0:tq] = jnp.where(visible, st_ref[:, c0:tq], MASK_VALUE)
            blk_max = jnp.max(st_ref[:, c0:tq], axis=0, keepdims=True)
        else:
            blk_max = bm[s][...]
        m_old = m_s[:, c0:]
        m_new = jnp.maximum(m_old, blk_max)
        m_s[:, c0:] = m_new
        al[s][:, c0:] = jnp.exp2(m_old - m_new)
        pb[s][:, c0:tq] = jnp.exp2(st_ref[:, c0:tq] - m_new).astype(BF16)

    def pv(j, s, c0=0):
        acc[:, c0:tq] = al[s][:, c0:] * acc[:, c0:tq] + jnp.dot(
            vt_ref[j], pb[s][:, c0:tq], preferred_element_type=F32)

    def trip(t, carry):
        a = KV_PER_Q * t
        score(a, 0)
        score(a + 1, 1)
        for d in range(KV_PER_Q):
            s = d % 2
            softmax(s)
            pv(jnp.maximum(a + d - 1, 0), 1 - s)
            if d + 2 < KV_PER_Q:
                score(a + d + 2, s)
        return carry

    lax.fori_loop(0, qi, trip, 0)
    a = KV_PER_Q * qi
    score(a, 0, with_max=False)
    score(a + 1, 1, c0=tk, with_max=False)
    for d in range(KV_PER_Q):
        s = d % 2
        softmax(s, c0=d * tk, diagonal=True)
        pv(jnp.maximum(a + d - 1, 0), 1 - s, c0=max(d - 1, 0) * tk)
        if d + 2 < KV_PER_Q:
            score(a + d + 2, s, c0=(d + 2) * tk, with_max=False)
    pv(a + KV_PER_Q - 1, (KV_PER_Q - 1) % 2, c0=(KV_PER_Q - 1) * tk)
    o_ref[...] = (acc[0:HEAD_DIM, 0:tq] / acc[HEAD_DIM:HEAD_DIM + 1, 0:tq]).T.astype(BF16)


def _fox_attn(qt, c, kaug, vt):
    bsz, seq = c.shape[0], c.shape[2]
    n_t, tk = vt.shape[2], vt.shape[4]
    tq = TQ_ATTN
    assert tq == KV_PER_Q * tk and KV_PER_Q % 2 == 0 and seq % tq == 0
    return pl.pallas_call(
        _fox_attn_kernel,
        grid=(bsz, N_HEADS, seq // tq),
        in_specs=[
            pl.BlockSpec((None, None, KV_PER_Q, HEAD_DIM, tk), lambda b, h, i: (b, h, i, 0, 0)),
            pl.BlockSpec((None, N_HEADS, tq), lambda b, h, i: (b, 0, i)),
            pl.BlockSpec((None, None, seq, AUG_DIM), lambda b, h, i: (b, h, 0, 0)),
            pl.BlockSpec((None, None, n_t, VT_ROWS, tk), lambda b, h, i: (b, h, 0, 0, 0)),
        ],
        out_specs=pl.BlockSpec((None, tq, HEAD_DIM), lambda b, h, i: (b, i, h)),
        out_shape=jax.ShapeDtypeStruct((bsz, seq, D_ATTN), BF16),
        scratch_shapes=[
            pltpu.VMEM((AUG_DIM, tq), BF16),
            pltpu.VMEM((tk, tq + SCRATCH_PAD), F32),
            pltpu.VMEM((tk, tq + SCRATCH_PAD), F32),
            pltpu.VMEM((tk, tq + SCRATCH_PAD), BF16),
            pltpu.VMEM((tk, tq + SCRATCH_PAD), BF16),
            pltpu.VMEM((1, tq), F32),
            pltpu.VMEM((1, tq), F32),
            pltpu.VMEM((1, tq), F32),
            pltpu.VMEM((1, tq), F32),
            pltpu.VMEM((1, tq), F32),
            pltpu.VMEM((VT_ROWS, tq + SCRATCH_PAD), F32),
        ],
        compiler_params=pltpu.CompilerParams(
            dimension_semantics=("arbitrary", "arbitrary", "arbitrary"),
            vmem_limit_bytes=_vmem_limit(56)),
        name="fox_attn",
    )(qt, c, kaug, vt)


def _out_ffn_kernel(x_ref, attn_ref, sga_ref, rnn_ref, woa_ref, wout_ref, g_ref, wg_ref, wu_ref,
                    wd_ref, o_ref):
    y_attn = jnp.dot(attn_ref[...], woa_ref[...], preferred_element_type=F32)
    merged = sga_ref[...] * y_attn + rnn_ref[...]
    x1 = x_ref[...] + jnp.dot(merged.astype(BF16), wout_ref[...], preferred_element_type=F32)
    h = _rms_norm(x1, g_ref[...]).astype(BF16)
    o_ref[...] = x1 + 0.5 * _swiglu(h, wg_ref, wu_ref, wd_ref)


def _out_ffn(x2d, attn2d, sga2d, rnn2d, layer, woa, wout, g, wg, wu, wd):
    n_tok = x2d.shape[0]
    tm = TM_OUT
    tile = pl.BlockSpec((tm, D_MODEL), lambda i: (i, 0))
    consts = (woa, wout, g, wg, wu, wd)
    return pl.pallas_call(
        _out_ffn_kernel,
        grid=(n_tok // tm,),
        in_specs=[tile, tile, tile, tile] + [_param_spec(a, layer) for a in consts],
        out_specs=tile,
        out_shape=jax.ShapeDtypeStruct((n_tok, D_MODEL), F32),
        compiler_params=pltpu.CompilerParams(
            dimension_semantics=("parallel",), vmem_limit_bytes=_vmem_limit(56)),
        name="out_ffn",
    )(x2d, attn2d, sga2d, rnn2d, *consts)


def _block_diag_tiles(w):
    per = V7X_MXU_DIM // RNN_BLOCK
    w4 = w.reshape(w.shape[0], N_RNN_BLOCKS // per, per, RNN_BLOCK, RNN_BLOCK)
    eye = jnp.eye(per, dtype=w.dtype)
    return jnp.einsum("ljarc,ab->ljarbc", w4, eye).reshape(
        w.shape[0], -1, V7X_MXU_DIM, V7X_MXU_DIM)


def _bias_selector():
    r = jnp.arange(V7X_LANES)[:, None]
    col = jnp.arange(N_HEADS * V7X_LANES)[None, :]
    head, pos = col // V7X_LANES, col % V7X_LANES
    ones_row = N_SPLIT * N_HEADS + head
    piece_row = (pos - N_SPLIT) * N_HEADS + head
    hit = jnp.where(pos < N_SPLIT, r == ones_row, (pos < 2 * N_SPLIT) & (r == piece_row))
    return hit.astype(BF16)


def kernel(x, ffn1_norm, ffn1_w_gate, ffn1_w_up, ffn1_w_down, mix_norm, w_in, b_in, q_norm, k_norm, conv_w, conv_b, lru_w_a, lru_b_a, lru_w_x, lru_b_x, lru_lambda, w_o_attn, w_o_rnn, w_out, ffn2_norm, ffn2_w_gate, ffn2_w_up, ffn2_w_down):
    bsz, seq, _ = x.shape
    depth = w_in.shape[0]
    n_tok = bsz * seq
    assert seq % T_ATTN == 0 and seq % TM_RNN == 0 and n_tok % TM_FFN == 0 and n_tok % TM_OUT == 0
    row = lambda v: v.reshape(depth, 1, -1).astype(F32)
    bf16 = lambda w: w.astype(BF16)
    sel = _bias_selector()
    f_lo, f_hi = 3 * D_ATTN, 3 * D_ATTN + N_HEADS
    ngrp = D_RNN // V7X_MXU_DIM

    ffn1 = (row(ffn1_norm), bf16(ffn1_w_gate), bf16(ffn1_w_up), bf16(ffn1_w_down))
    wft = jnp.zeros((depth, 2 * V7X_SUBLANES, D_MODEL), BF16).at[:, :N_HEADS].set(
        bf16(jnp.swapaxes(w_in[:, :, f_lo:f_hi], 1, 2)))
    attn_w = (row(mix_norm), bf16(w_in[:, :, :f_lo]), row(b_in[:, :f_lo]), wft,
              b_in[:, f_lo:f_hi].reshape(depth, N_HEADS, 1), row(q_norm), row(k_norm))
    wr, br = bf16(w_in[:, :, f_hi:]), b_in[:, f_hi:]
    wxg = wr[:, :, :2 * D_RNN].reshape(depth, D_MODEL, 2, ngrp, V7X_MXU_DIM).transpose(
        0, 3, 1, 2, 4).reshape(depth, ngrp, D_MODEL, 2 * V7X_MXU_DIM)
    bxg = br[:, :2 * D_RNN].reshape(depth, 2, ngrp, V7X_MXU_DIM).transpose(0, 2, 1, 3).reshape(
        depth, ngrp, 1, 2 * V7X_MXU_DIM)
    wgate = bf16(jnp.concatenate([_block_diag_tiles(lru_w_a), _block_diag_tiles(lru_w_x)], axis=3))
    rnn_w = (wxg, bxg, wr[:, :, 2 * D_RNN:], row(br[:, 2 * D_RNN:]), conv_w.astype(F32),
             row(conv_b), wgate, row(lru_b_a), row(lru_b_x), row(lru_lambda), bf16(w_o_rnn))
    out_w = (bf16(w_o_attn), bf16(w_out), row(ffn2_norm), bf16(ffn2_w_gate), bf16(ffn2_w_up),
             bf16(ffn2_w_down))

    for l in range(depth):
        x = _ffn(x.reshape(n_tok, D_MODEL), l, *ffn1).reshape(bsz, seq, D_MODEL)
        qt, kaug, vt, c = _attn_proj(x, l, *attn_w, sel)
        sga, rnn = _rnn_branch(x, l, attn_w[0], *rnn_w)
        attn = _fox_attn(qt, c, kaug, vt)
        x = _out_ffn(
            x.reshape(n_tok, D_MODEL), attn.reshape(n_tok, D_ATTN), sga.reshape(n_tok, D_MODEL),
            rnn.reshape(n_tok, D_MODEL), l, *out_w).reshape(bsz, seq, D_MODEL)
    return x
```

```python
import jax
import jax.numpy as jnp
from jax import lax
from jax.experimental import pallas as pl
from jax.experimental.pallas import tpu as pltpu

F32 = jnp.float32
BF16 = jnp.bfloat16

D_MODEL = 1024
N_HEADS = 8
HEAD_DIM = 128
D_ATTN = N_HEADS * HEAD_DIM
D_RNN = D_MODEL
N_RNN_BLOCKS = 16
RNN_BLOCK = D_RNN // N_RNN_BLOCKS
CONV_WIDTH = 4
LRU_C = 8.0
D_FF = 2816
NORM_EPS = 1e-6

V7X_LANES = 128
V7X_SUBLANES = 8
V7X_MXU_DIM = 256
V7X_VMEM_BYTES = 64 * 1024 * 1024

AUG_DIM = HEAD_DIM + V7X_LANES
N_SPLIT = 3

FF_CHUNKS = ((0, 1024), (1024, 2048), (2048, D_FF))

TM_FFN = 1024
TM_OUT = 512
TM_RNN = 512
T_ATTN = 512
KV_PER_Q = 4
TQ_ATTN = KV_PER_Q * T_ATTN
LONG_TRIP_Q = 2
SCRATCH_PAD = 2 * V7X_LANES
VT_ROWS = HEAD_DIM + 16

MASK_VALUE = -1e30
LOG2_E = 1.4426950408889634


def _vmem_limit(mib):
    return min(mib * 1024 * 1024, V7X_VMEM_BYTES - 4 * 1024 * 1024)


def _param_spec(arr, layer=None):
    if layer is None:
        block, idx = arr.shape, (0,) * arr.ndim
    else:
        block, idx = (None,) + arr.shape[1:], (layer,) + (0,) * (arr.ndim - 1)
    return pl.BlockSpec(block, lambda *_: idx, pipeline_mode=pl.Buffered(1))


def _rms_norm(x, g):
    return x * lax.rsqrt(jnp.mean(x * x, axis=-1, keepdims=True) + NORM_EPS) * g


def _swiglu(h_bf, wg_ref, wu_ref, wd_ref):
    acc = None
    for lo, hi in FF_CHUNKS:
        g = jnp.dot(h_bf, wg_ref[:, lo:hi], preferred_element_type=F32)
        u = jnp.dot(h_bf, wu_ref[:, lo:hi], preferred_element_type=F32)
        a = (g * jax.nn.sigmoid(g) * u).astype(BF16)
        part = jnp.dot(a, wd_ref[lo:hi, :], preferred_element_type=F32)
        acc = part if acc is None else acc + part
    return acc


def _ffn_kernel(x_ref, g_ref, wg_ref, wu_ref, wd_ref, o_ref):
    x = x_ref[...]
    h = _rms_norm(x, g_ref[...]).astype(BF16)
    o_ref[...] = x + 0.5 * _swiglu(h, wg_ref, wu_ref, wd_ref)


def _ffn(x2d, layer, g, wg, wu, wd):
    n_tok = x2d.shape[0]
    tm = TM_FFN
    tile = pl.BlockSpec((tm, D_MODEL), lambda i: (i, 0))
    return pl.pallas_call(
        _ffn_kernel,
        grid=(n_tok // tm,),
        in_specs=[tile] + [_param_spec(a, layer) for a in (g, wg, wu, wd)],
        out_specs=tile,
        out_shape=jax.ShapeDtypeStruct((n_tok, D_MODEL), F32),
        compiler_params=pltpu.CompilerParams(
            dimension_semantics=("parallel",), vmem_limit_bytes=_vmem_limit(48)),
        name="ffn",
    )(x2d, g, wg, wu, wd)


def _split3(c):
    hi = c.astype(BF16).astype(F32)
    r = c - hi
    mid = r.astype(BF16).astype(F32)
    lo = (r - mid).astype(BF16).astype(F32)
    return hi, mid, lo


def _attn_proj_kernel(x_ref, g_ref, wqkv_ref, bqkv_ref, wft_ref, bf_ref, qn_ref, kn_ref, sel_ref,
                      qt_ref, kaug_ref, vt_ref, c_ref, c_carry):
    tm = x_ref.shape[0]

    @pl.when(pl.program_id(1) == 0)
    def _():
        c_carry[...] = jnp.zeros_like(c_carry)

    h = _rms_norm(x_ref[...], g_ref[...]).astype(BF16)

    fl = lax.dot_general(wft_ref[...], h, (((1,), (1,)), ((), ())), preferred_element_type=F32)
    fl = fl[:N_HEADS, :] + bf_ref[...]
    log_f = jnp.minimum(fl, 0.0) - jnp.log1p(jnp.exp(-jnp.abs(fl)))

    lane = lax.broadcasted_iota(jnp.int32, log_f.shape, 1)
    cs = log_f
    d = 1
    while d < tm:
        cs = cs + jnp.where(lane >= d, pltpu.roll(cs, d, axis=1), 0.0)
        d *= 2
    c = cs + c_carry[:, 0:1]
    c_carry[...] = jnp.broadcast_to(c[:, tm - 1:tm], c_carry.shape)
    c_ref[...] = c

    hi, mid, lo = _split3(c * LOG2_E)
    ones = jnp.ones_like(c)
    pad = jnp.zeros((V7X_LANES - (N_SPLIT + 1) * N_HEADS, tm), F32)
    stack = jnp.concatenate([-hi, -mid, -lo, ones, pad], axis=0)
    stack_t = stack.T.astype(BF16)

    qkv = jnp.dot(h, wqkv_ref[...], preferred_element_type=F32) + bqkv_ref[...]
    extras = jnp.dot(stack_t, sel_ref[...], preferred_element_type=F32)

    scale = HEAD_DIM ** -0.5 * LOG2_E
    for hh in range(N_HEADS):
        sl = slice(hh * HEAD_DIM, (hh + 1) * HEAD_DIM)
        qh = qkv[:, sl]
        qh = qh * lax.rsqrt(jnp.mean(qh * qh, axis=-1, keepdims=True) + NORM_EPS) * qn_ref[...]
        qt_ref[hh] = (qh * scale).T.astype(BF16)
        kh = qkv[:, D_ATTN + hh * HEAD_DIM:D_ATTN + (hh + 1) * HEAD_DIM]
        kh = kh * lax.rsqrt(jnp.mean(kh * kh, axis=-1, keepdims=True) + NORM_EPS) * kn_ref[...]
        kaug_ref[hh, :, 0:HEAD_DIM] = kh.astype(BF16)
        kaug_ref[hh, :, HEAD_DIM:AUG_DIM] = extras[:, sl].astype(BF16)
        vh = qkv[:, 2 * D_ATTN + hh * HEAD_DIM:2 * D_ATTN + (hh + 1) * HEAD_DIM]
        vt_ref[hh, 0:HEAD_DIM, :] = vh.T.astype(BF16)
        vt_ref[hh, HEAD_DIM:VT_ROWS, :] = jnp.ones((VT_ROWS - HEAD_DIM, tm), BF16)


def _attn_proj(x, layer, g, wqkv, bqkv, wft, bf, qn, kn, sel):
    bsz, seq, _ = x.shape
    tm = T_ATTN
    n_t = seq // tm
    out_shape = (
        jax.ShapeDtypeStruct((bsz, N_HEADS, n_t, HEAD_DIM, tm), BF16),
        jax.ShapeDtypeStruct((bsz, N_HEADS, seq, AUG_DIM), BF16),
        jax.ShapeDtypeStruct((bsz, N_HEADS, n_t, VT_ROWS, tm), BF16),
        jax.ShapeDtypeStruct((bsz, N_HEADS, seq), F32),
    )
    out_specs = (
        pl.BlockSpec((None, N_HEADS, None, HEAD_DIM, tm), lambda b, t: (b, 0, t, 0, 0)),
        pl.BlockSpec((None, N_HEADS, tm, AUG_DIM), lambda b, t: (b, 0, t, 0)),
        pl.BlockSpec((None, N_HEADS, None, VT_ROWS, tm), lambda b, t: (b, 0, t, 0, 0)),
        pl.BlockSpec((None, N_HEADS, tm), lambda b, t: (b, 0, t)),
    )
    stacked = (g, wqkv, bqkv, wft, bf, qn, kn)
    consts = stacked + (sel,)
    return pl.pallas_call(
        _attn_proj_kernel,
        grid=(bsz, n_t),
        in_specs=[pl.BlockSpec((None, tm, D_MODEL), lambda b, t: (b, t, 0))]
        + [_param_spec(a, layer) for a in stacked] + [_param_spec(sel)],
        out_specs=out_specs,
        out_shape=out_shape,
        scratch_shapes=[pltpu.VMEM((N_HEADS, V7X_LANES), F32)],
        compiler_params=pltpu.CompilerParams(
            dimension_semantics=("arbitrary", "arbitrary"), vmem_limit_bytes=_vmem_limit(48)),
        name="attn_proj",
    )(x, *consts)


def _gelu_tanh(x):
    return 0.5 * x * (1.0 + jnp.tanh(0.7978845608028654 * (x + 0.044715 * (x * x * x))))


def _sigmoid(x):
    return 0.5 * jnp.tanh(0.5 * x) + 0.5


def _rnn_kernel(x_ref, g_ref, perm_ref, unperm_ref, wxg_ref, bxg_ref, wm_ref, bm_ref, cw_ref,
                cb_ref, wgate_ref, ba_ref, bx_ref, lam_ref, wo_ref, sga_ref, rnn_ref, tail, h_carry):
    tm = x_ref.shape[0]
    sub = V7X_SUBLANES
    gw = V7X_MXU_DIM
    ngrp = D_RNN // gw
    nv = tm // sub
    ntail = CONV_WIDTH - 1
    mw = 2 * D_MODEL // ngrp

    @pl.when(pl.program_id(1) == 0)
    def _():
        tail[...] = jnp.zeros_like(tail)
        h_carry[...] = jnp.zeros_like(h_carry)

    sub_id = lax.broadcasted_iota(jnp.int32, (sub, gw), 0)
    first = sub_id == 0

    def vreg(arr, r):
        return arr[r * sub:(r + 1) * sub, :]

    h = _rms_norm(x_ref[...], g_ref[...]).astype(BF16)
    hp = jnp.dot(perm_ref[...], h, preferred_element_type=F32).astype(BF16)

    def xg_proj(j):
        return jnp.dot(hp, wxg_ref[j], preferred_element_type=F32) + bxg_ref[j]

    def merge_gate(j):
        lo, hi = j * mw, (j + 1) * mw
        return _sigmoid(jnp.dot(h, wm_ref[:, lo:hi], preferred_element_type=F32) + bm_ref[:, lo:hi])

    def conv_stage(j, xr):
        lo, hi = j * gw, (j + 1) * gw
        conv = cb_ref[:, lo:hi]
        for tap in range(CONV_WIDTH):
            k = CONV_WIDTH - 1 - tap
            if k == 0:
                shifted = xr
            else:
                head = []
                for i in range(k):
                    prev = pltpu.roll(tail[(ntail - k + i) * sub:(ntail - k + i + 1) * sub, lo:hi],
                                      1, axis=0)
                    cur = pltpu.roll(vreg(xr, nv - k + i), 1, axis=0)
                    head.append(jnp.where(first, prev, cur))
                shifted = jnp.concatenate(head + [xr[:(nv - k) * sub, :]], axis=0)
            conv = conv + shifted * cw_ref[tap:tap + 1, lo:hi]
        tail[:, lo:hi] = xr[(nv - ntail) * sub:, :]
        return conv

    def lru_stage(j, gz, conv, gr):
        lo, hi = j * gw, (j + 1) * gw
        r_gate = _sigmoid(gz[:, :gw] + ba_ref[:, lo:hi])
        i_gate = _sigmoid(gz[:, gw:] + bx_ref[:, lo:hi])
        nlam = -lam_ref[:, lo:hi]
        softplus = jnp.maximum(nlam, 0.0) + jnp.log1p(jnp.exp(-jnp.abs(nlam)))
        log_a = (-LRU_C * softplus) * r_gate
        a = jnp.exp(log_a)
        om = -jnp.tanh(log_a) * (a * a + 1.0)
        u = jnp.where(om == 0.0, 0.0, om * lax.rsqrt(om)) * (i_gate * conv)

        h_loc, a_cum = [vreg(u, 0)], [vreg(a, 0)]
        for r in range(1, nv):
            ar = vreg(a, r)
            h_loc.append(ar * h_loc[-1] + vreg(u, r))
            a_cum.append(ar * a_cum[-1])
        aa, bb = a_cum[-1], h_loc[-1]
        d = 1
        while d < sub:
            keep = sub_id >= d
            a_sh = jnp.where(keep, pltpu.roll(aa, d, axis=0), 1.0)
            b_sh = jnp.where(keep, pltpu.roll(bb, d, axis=0), 0.0)
            bb = aa * b_sh + bb
            aa = aa * a_sh
            d *= 2
        h_in = h_carry[:, lo:hi]
        chunk_end = aa * h_in + bb
        entry = jnp.where(first, h_in, pltpu.roll(chunk_end, 1, axis=0))
        h_carry[:, lo:hi] = jnp.broadcast_to(chunk_end[sub - 1:sub, :], (sub, gw))
        hs = jnp.concatenate([h_loc[r] + a_cum[r] * entry for r in range(nv)], axis=0)
        return (hs * _gelu_tanh(gr)).astype(BF16)

    def out_stage(j, yr):
        yr = jnp.dot(unperm_ref[...], yr, preferred_element_type=F32).astype(BF16)
        return jnp.dot(yr, wo_ref[j * gw:(j + 1) * gw, :], preferred_element_type=F32)

    def gate_stage(j, conv):
        return jnp.dot(conv.astype(BF16), wgate_ref[j], preferred_element_type=F32)

    mg = {0: merge_gate(0)}
    rest = {0: xg_proj(0), 1: xg_proj(1)}
    conv = {0: conv_stage(0, rest[0][:, :gw])}
    gz = {0: gate_stage(0, conv[0])}
    mg[1] = merge_gate(1)
    rest[2] = xg_proj(2)
    conv[1] = conv_stage(1, rest[1][:, :gw])
    yr = {0: lru_stage(0, gz[0], conv[0], rest[0][:, gw:])}
    gz[1] = gate_stage(1, conv[1])
    y_rnn = out_stage(0, yr[0])
    rest[3] = xg_proj(3)
    conv[2] = conv_stage(2, rest[2][:, :gw])
    yr[1] = lru_stage(1, gz[1], conv[1], rest[1][:, gw:])
    gz[2] = gate_stage(2, conv[2])
    y_rnn = y_rnn + out_stage(1, yr[1])
    conv[3] = conv_stage(3, rest[3][:, :gw])
    yr[2] = lru_stage(2, gz[2], conv[2], rest[2][:, gw:])
    gz[3] = gate_stage(3, conv[3])
    mg[2] = merge_gate(2)
    y_rnn = y_rnn + out_stage(2, yr[2])
    yr[3] = lru_stage(3, gz[3], conv[3], rest[3][:, gw:])
    mg[3] = merge_gate(3)
    y_rnn = y_rnn + out_stage(3, yr[3])

    sga_ref[...] = jnp.concatenate([mg[0], mg[1]], axis=1)
    rnn_ref[...] = jnp.concatenate([mg[2], mg[3]], axis=1) * y_rnn


def _time_permutation(tm):
    rho = jnp.arange(tm)
    src = (tm // V7X_SUBLANES) * (rho % V7X_SUBLANES) + rho // V7X_SUBLANES
    return (src[:, None] == jnp.arange(tm)[None, :]).astype(BF16)


def _rnn_branch(x, layer, g, wxg, bxg, wm, bm, cw, cb, wgate, ba, bx, lam, wo):
    bsz, seq, _ = x.shape
    tm = TM_RNN
    tile = pl.BlockSpec((None, tm, D_MODEL), lambda b, t: (b, t, 0))
    perm = _time_permutation(tm)
    shared = (perm, perm.T)
    stacked = (wxg, bxg, wm, bm, cw, cb, wgate, ba, bx, lam, wo)
    consts = (g,) + shared + stacked
    out = jax.ShapeDtypeStruct((bsz, seq, D_MODEL), F32)
    return pl.pallas_call(
        _rnn_kernel,
        grid=(bsz, seq // tm),
        in_specs=[tile, _param_spec(g, layer)] + [_param_spec(a) for a in shared]
        + [_param_spec(a, layer) for a in stacked],
        out_specs=(tile, tile),
        out_shape=(out, out),
        scratch_shapes=[
            pltpu.VMEM(((CONV_WIDTH - 1) * V7X_SUBLANES, D_RNN), F32),
            pltpu.VMEM((V7X_SUBLANES, D_RNN), F32),
        ],
        compiler_params=pltpu.CompilerParams(
            dimension_semantics=("arbitrary", "arbitrary"), vmem_limit_bytes=_vmem_limit(48)),
        name="rnn_branch",
    )(x, *consts)


def _fox_attn_kernel(qt_ref, c_ref, k_ref, vt_ref, o_ref, qt_aug, st0, st1, p0, p1, al0, al1,
                     bm0, bm1, m_s, acc):
    tq = o_ref.shape[0]
    tk = vt_ref.shape[2]
    hh = pl.program_id(1)
    qi = pl.program_id(2)
    st, pb, al, bm = (st0, st1), (p0, p1), (al0, al1), (bm0, bm1)

    for ci in range(qt_ref.shape[0]):
        qt_aug[0:HEAD_DIM, ci * tk:(ci + 1) * tk] = qt_ref[ci]
    hi, mid, lo = _split3(c_ref[pl.ds(hh, 1), :] * LOG2_E)
    row = lax.broadcasted_iota(jnp.int32, (V7X_SUBLANES, tq), 0)
    top = jnp.where(row == 0, hi, jnp.where(row == 1, mid, jnp.where(row == 2, lo,
          jnp.where(row < 2 * N_SPLIT, 1.0, 0.0))))
    qt_aug[HEAD_DIM:AUG_DIM, :] = jnp.concatenate(
        [top, jnp.zeros((V7X_LANES - V7X_SUBLANES, tq), F32)], axis=0).astype(BF16)

    m_s[...] = jnp.full(m_s.shape, MASK_VALUE, F32)
    acc[...] = jnp.zeros_like(acc)
    p1[...] = jnp.zeros_like(p1)
    al1[...] = jnp.ones_like(al1)

    def score(j, s, c0=0, with_max=True):
        kb = k_ref[pl.ds(pl.multiple_of(j * tk, tk), tk), :]
        blk = jnp.dot(kb, qt_aug[:, c0:], preferred_element_type=F32)
        st[s][:, c0:tq] = blk
        if with_max:
            bm[s][...] = jnp.max(blk, axis=0, keepdims=True)

    def softmax(s, c0=0, diagonal=False):
        st_ref = st[s]
        if diagonal:
            shape = (tk, tq - c0)
            visible = (lax.broadcasted_iota(jnp.int32, shape, 0)
                       <= lax.broadcasted_iota(jnp.int32, shape, 1))
            st_ref[:, c0:tq] = jnp.where(visible, st_ref[:, c0:tq], MASK_VALUE)
            blk_max = jnp.max(st_ref[:, c0:tq], axis=0, keepdims=True)
        else:
            blk_max = bm[s][...]
        m_old = m_s[:, c0:]
        m_new = jnp.maximum(m_old, blk_max)
        m_s[:, c0:] = m_new
        al[s][:, c0:] = jnp.exp2(m_old - m_new)
        pb[s][:, c0:tq] = jnp.exp2(st_ref[:, c0:tq] - m_new).astype(BF16)

    def pv(j, s, c0=0):
        acc[:, c0:tq] = al[s][:, c0:] * acc[:, c0:tq] + jnp.dot(
            vt_ref[j], pb[s][:, c0:tq], preferred_element_type=F32)

    def visible_blocks(a, nblk):
        score(a, 0)
        score(a + 1, 1)
        for d in range(nblk):
            s = d % 2
            softmax(s)
            pv(jnp.maximum(a + d - 1, 0), 1 - s)
            if d + 2 < nblk:
                score(a + d + 2, s)

    long_blk = LONG_TRIP_Q * KV_PER_Q
    n_long = qi // LONG_TRIP_Q

    def long_trip(t, carry):
        visible_blocks(long_blk * t, long_blk)
        return carry

    def short_trip(t, carry):
        visible_blocks(long_blk * n_long + KV_PER_Q * t, KV_PER_Q)
        return carry

    lax.fori_loop(0, n_long, long_trip, 0)
    lax.fori_loop(0, qi - LONG_TRIP_Q * n_long, short_trip, 0)
    a = KV_PER_Q * qi
    score(a, 0, with_max=False)
    score(a + 1, 1, c0=tk, with_max=False)
    for d in range(KV_PER_Q):
        s = d % 2
        softmax(s, c0=d * tk, diagonal=True)
        pv(jnp.maximum(a + d - 1, 0), 1 - s, c0=max(d - 1, 0) * tk)
        if d + 2 < KV_PER_Q:
            score(a + d + 2, s, c0=(d + 2) * tk, with_max=False)
    pv(a + KV_PER_Q - 1, (KV_PER_Q - 1) % 2, c0=(KV_PER_Q - 1) * tk)
    o_ref[...] = (acc[0:HEAD_DIM, 0:tq] / acc[HEAD_DIM:HEAD_DIM + 1, 0:tq]).T.astype(BF16)


def _fox_attn(qt, c, kaug, vt):
    bsz, seq = c.shape[0], c.shape[2]
    n_t, tk = vt.shape[2], vt.shape[4]
    tq = TQ_ATTN
    assert tq == KV_PER_Q * tk and KV_PER_Q % 2 == 0 and seq % tq == 0
    return pl.pallas_call(
        _fox_attn_kernel,
        grid=(bsz, N_HEADS, seq // tq),
        in_specs=[
            pl.BlockSpec((None, None, KV_PER_Q, HEAD_DIM, tk), lambda b, h, i: (b, h, i, 0, 0)),
            pl.BlockSpec((None, N_HEADS, tq), lambda b, h, i: (b, 0, i)),
            pl.BlockSpec((None, None, seq, AUG_DIM), lambda b, h, i: (b, h, 0, 0)),
            pl.BlockSpec((None, None, n_t, VT_ROWS, tk), lambda b, h, i: (b, h, 0, 0, 0)),
        ],
        out_specs=pl.BlockSpec((None, tq, HEAD_DIM), lambda b, h, i: (b, i, h)),
        out_shape=jax.ShapeDtypeStruct((bsz, seq, D_ATTN), BF16),
        scratch_shapes=[
            pltpu.VMEM((AUG_DIM, tq), BF16),
            pltpu.VMEM((tk, tq + SCRATCH_PAD), F32),
            pltpu.VMEM((tk, tq + SCRATCH_PAD), F32),
            pltpu.VMEM((tk, tq + SCRATCH_PAD), BF16),
            pltpu.VMEM((tk, tq + SCRATCH_PAD), BF16),
            pltpu.VMEM((1, tq), F32),
            pltpu.VMEM((1, tq), F32),
            pltpu.VMEM((1, tq), F32),
            pltpu.VMEM((1, tq), F32),
            pltpu.VMEM((1, tq), F32),
            pltpu.VMEM((VT_ROWS, tq + SCRATCH_PAD), F32),
        ],
        compiler_params=pltpu.CompilerParams(
            dimension_semantics=("arbitrary", "arbitrary", "arbitrary"),
            vmem_limit_bytes=_vmem_limit(56)),
        name="fox_attn",
    )(qt, c, kaug, vt)


def _out_ffn_kernel(x_ref, attn_ref, sga_ref, rnn_ref, woa_ref, wout_ref, g_ref, wg_ref, wu_ref,
                    wd_ref, o_ref):
    y_attn = jnp.dot(attn_ref[...], woa_ref[...], preferred_element_type=F32)
    merged = sga_ref[...] * y_attn + rnn_ref[...]
    x1 = x_ref[...] + jnp.dot(merged.astype(BF16), wout_ref[...], preferred_element_type=F32)
    h = _rms_norm(x1, g_ref[...]).astype(BF16)
    o_ref[...] = x1 + 0.5 * _swiglu(h, wg_ref, wu_ref, wd_ref)


def _out_ffn(x2d, attn2d, sga2d, rnn2d, layer, woa, wout, g, wg, wu, wd):
    n_tok = x2d.shape[0]
    tm = TM_OUT
    tile = pl.BlockSpec((tm, D_MODEL), lambda i: (i, 0))
    consts = (woa, wout, g, wg, wu, wd)
    return pl.pallas_call(
        _out_ffn_kernel,
        grid=(n_tok // tm,),
        in_specs=[tile, tile, tile, tile] + [_param_spec(a, layer) for a in consts],
        out_specs=tile,
        out_shape=jax.ShapeDtypeStruct((n_tok, D_MODEL), F32),
        compiler_params=pltpu.CompilerParams(
            dimension_semantics=("parallel",), vmem_limit_bytes=_vmem_limit(56)),
        name="out_ffn",
    )(x2d, attn2d, sga2d, rnn2d, *consts)


def _block_diag_tiles(w):
    per = V7X_MXU_DIM // RNN_BLOCK
    w4 = w.reshape(w.shape[0], N_RNN_BLOCKS // per, per, RNN_BLOCK, RNN_BLOCK)
    eye = jnp.eye(per, dtype=w.dtype)
    return jnp.einsum("ljarc,ab->ljarbc", w4, eye).reshape(
        w.shape[0], -1, V7X_MXU_DIM, V7X_MXU_DIM)


def _bias_selector():
    r = jnp.arange(V7X_LANES)[:, None]
    col = jnp.arange(N_HEADS * V7X_LANES)[None, :]
    head, pos = col // V7X_LANES, col % V7X_LANES
    ones_row = N_SPLIT * N_HEADS + head
    piece_row = (pos - N_SPLIT) * N_HEADS + head
    hit = jnp.where(pos < N_SPLIT, r == ones_row, (pos < 2 * N_SPLIT) & (r == piece_row))
    return hit.astype(BF16)


def kernel(x, ffn1_norm, ffn1_w_gate, ffn1_w_up, ffn1_w_down, mix_norm, w_in, b_in, q_norm, k_norm, conv_w, conv_b, lru_w_a, lru_b_a, lru_w_x, lru_b_x, lru_lambda, w_o_attn, w_o_rnn, w_out, ffn2_norm, ffn2_w_gate, ffn2_w_up, ffn2_w_down):
    bsz, seq, _ = x.shape
    depth = w_in.shape[0]
    n_tok = bsz * seq
    assert seq % T_ATTN == 0 and seq % TM_RNN == 0 and n_tok % TM_FFN == 0 and n_tok % TM_OUT == 0
    row = lambda v: v.reshape(depth, 1, -1).astype(F32)
    bf16 = lambda w: w.astype(BF16)
    sel = _bias_selector()
    f_lo, f_hi = 3 * D_ATTN, 3 * D_ATTN + N_HEADS
    ngrp = D_RNN // V7X_MXU_DIM

    ffn1 = (row(ffn1_norm), bf16(ffn1_w_gate), bf16(ffn1_w_up), bf16(ffn1_w_down))
    wft = jnp.zeros((depth, 2 * V7X_SUBLANES, D_MODEL), BF16).at[:, :N_HEADS].set(
        bf16(jnp.swapaxes(w_in[:, :, f_lo:f_hi], 1, 2)))
    attn_w = (row(mix_norm), bf16(w_in[:, :, :f_lo]), row(b_in[:, :f_lo]), wft,
              b_in[:, f_lo:f_hi].reshape(depth, N_HEADS, 1), row(q_norm), row(k_norm))
    wr, br = bf16(w_in[:, :, f_hi:]), b_in[:, f_hi:]
    wxg = wr[:, :, :2 * D_RNN].reshape(depth, D_MODEL, 2, ngrp, V7X_MXU_DIM).transpose(
        0, 3, 1, 2, 4).reshape(depth, ngrp, D_MODEL, 2 * V7X_MXU_DIM)
    bxg = br[:, :2 * D_RNN].reshape(depth, 2, ngrp, V7X_MXU_DIM).transpose(0, 2, 1, 3).reshape(
        depth, ngrp, 1, 2 * V7X_MXU_DIM)
    wgate = bf16(jnp.concatenate([_block_diag_tiles(lru_w_a), _block_diag_tiles(lru_w_x)], axis=3))
    rnn_w = (wxg, bxg, wr[:, :, 2 * D_RNN:], row(br[:, 2 * D_RNN:]), conv_w.astype(F32),
             row(conv_b), wgate, row(lru_b_a), row(lru_b_x), row(lru_lambda), bf16(w_o_rnn))
    out_w = (bf16(w_o_attn), bf16(w_out), row(ffn2_norm), bf16(ffn2_w_gate), bf16(ffn2_w_up),
             bf16(ffn2_w_down))

    for l in range(depth):
        x = _ffn(x.reshape(n_tok, D_MODEL), l, *ffn1).reshape(bsz, seq, D_MODEL)
        qt, kaug, vt, c = _attn_proj(x, l, *attn_w, sel)
        sga, rnn = _rnn_branch(x, l, attn_w[0], *rnn_w)
        attn = _fox_attn(qt, c, kaug, vt)
        x = _out_ffn(
            x.reshape(n_tok, D_MODEL), attn.reshape(n_tok, D_ATTN), sga.reshape(n_tok, D_MODEL),
            rnn.reshape(n_tok, D_MODEL), l, *out_w).reshape(bsz, seq, D_MODEL)
    return x
```

```python
import jax
import jax.numpy as jnp
from jax import lax
from jax.experimental import pallas as pl
from jax.experimental.pallas import tpu as pltpu

F32 = jnp.float32
BF16 = jnp.bfloat16

D_MODEL = 1024
N_HEADS = 8
HEAD_DIM = 128
D_ATTN = N_HEADS * HEAD_DIM
D_RNN = D_MODEL
N_RNN_BLOCKS = 16
RNN_BLOCK = D_RNN // N_RNN_BLOCKS
CONV_WIDTH = 4
LRU_C = 8.0
D_FF = 2816
NORM_EPS = 1e-6

V7X_LANES = 128
V7X_SUBLANES = 8
V7X_MXU_DIM = 256
V7X_VMEM_BYTES = 64 * 1024 * 1024

AUG_DIM = HEAD_DIM + V7X_LANES
N_SPLIT = 3

FF_CHUNKS = ((0, 1024), (1024, 2048), (2048, D_FF))

TM_FFN = 1024
TM_OUT = 512
TM_RNN = 512
T_ATTN = 512
KV_PER_Q = 4
TQ_ATTN = KV_PER_Q * T_ATTN
LONG_TRIP_Q = 2
SCRATCH_PAD = 2 * V7X_LANES
VT_ROWS = HEAD_DIM + 16

MASK_VALUE = -1e30
LOG2_E = 1.4426950408889634


MIB = 1024 * 1024
VMEM_TILE_CALLS_MIB = 48
VMEM_LARGE_CALLS_MIB = 56


def _vmem_limit(mib):
    assert mib * MIB < V7X_VMEM_BYTES
    return mib * MIB


def _param_spec(arr, layer=None):
    if layer is None:
        block, idx = arr.shape, (0,) * arr.ndim
    else:
        block, idx = (None,) + arr.shape[1:], (layer,) + (0,) * (arr.ndim - 1)
    return pl.BlockSpec(block, lambda *_: idx, pipeline_mode=pl.Buffered(1))


def _rms_norm(x, g):
    return x * lax.rsqrt(jnp.mean(x * x, axis=-1, keepdims=True) + NORM_EPS) * g


def _swiglu(h_bf, wg_ref, wu_ref, wd_ref):
    acc = None
    for lo, hi in FF_CHUNKS:
        g = jnp.dot(h_bf, wg_ref[:, lo:hi], preferred_element_type=F32)
        u = jnp.dot(h_bf, wu_ref[:, lo:hi], preferred_element_type=F32)
        a = (g * jax.nn.sigmoid(g) * u).astype(BF16)
        part = jnp.dot(a, wd_ref[lo:hi, :], preferred_element_type=F32)
        acc = part if acc is None else acc + part
    return acc


def _ffn_kernel(x_ref, g_ref, wg_ref, wu_ref, wd_ref, o_ref):
    x = x_ref[...]
    h = _rms_norm(x, g_ref[...]).astype(BF16)
    o_ref[...] = x + 0.5 * _swiglu(h, wg_ref, wu_ref, wd_ref)


def _ffn(x2d, layer, g, wg, wu, wd):
    n_tok = x2d.shape[0]
    tm = TM_FFN
    tile = pl.BlockSpec((tm, D_MODEL), lambda i: (i, 0))
    return pl.pallas_call(
        _ffn_kernel,
        grid=(n_tok // tm,),
        in_specs=[tile] + [_param_spec(a, layer) for a in (g, wg, wu, wd)],
        out_specs=tile,
        out_shape=jax.ShapeDtypeStruct((n_tok, D_MODEL), F32),
        compiler_params=pltpu.CompilerParams(
            dimension_semantics=("parallel",), vmem_limit_bytes=_vmem_limit(VMEM_TILE_CALLS_MIB)),
        name="ffn",
    )(x2d, g, wg, wu, wd)


def _split3(c):
    hi = c.astype(BF16).astype(F32)
    r = c - hi
    mid = r.astype(BF16).astype(F32)
    lo = (r - mid).astype(BF16).astype(F32)
    return hi, mid, lo


def _attn_proj_kernel(x_ref, g_ref, wqkv_ref, bqkv_ref, wft_ref, bf_ref, qn_ref, kn_ref, sel_ref,
                      qt_ref, kaug_ref, vt_ref, c_ref, c_carry):
    tm = x_ref.shape[0]

    @pl.when(pl.program_id(1) == 0)
    def _():
        c_carry[...] = jnp.zeros_like(c_carry)

    h = _rms_norm(x_ref[...], g_ref[...]).astype(BF16)

    fl = lax.dot_general(wft_ref[...], h, (((1,), (1,)), ((), ())), preferred_element_type=F32)
    fl = fl[:N_HEADS, :] + bf_ref[...]
    log_f = jnp.minimum(fl, 0.0) - jnp.log1p(jnp.exp(-jnp.abs(fl)))

    lane = lax.broadcasted_iota(jnp.int32, log_f.shape, 1)
    cs = log_f
    d = 1
    while d < tm:
        cs = cs + jnp.where(lane >= d, pltpu.roll(cs, d, axis=1), 0.0)
        d *= 2
    c = cs + c_carry[:, 0:1]
    c_carry[...] = jnp.broadcast_to(c[:, tm - 1:tm], c_carry.shape)
    c_ref[...] = c

    hi, mid, lo = _split3(c * LOG2_E)
    ones = jnp.ones_like(c)
    pad = jnp.zeros((V7X_LANES - (N_SPLIT + 1) * N_HEADS, tm), F32)
    stack = jnp.concatenate([-hi, -mid, -lo, ones, pad], axis=0)
    stack_t = stack.T.astype(BF16)

    qkv = jnp.dot(h, wqkv_ref[...], preferred_element_type=F32) + bqkv_ref[...]
    extras = jnp.dot(stack_t, sel_ref[...], preferred_element_type=F32)

    scale = HEAD_DIM ** -0.5 * LOG2_E
    for hh in range(N_HEADS):
        sl = slice(hh * HEAD_DIM, (hh + 1) * HEAD_DIM)
        qh = qkv[:, sl]
        qh = qh * lax.rsqrt(jnp.mean(qh * qh, axis=-1, keepdims=True) + NORM_EPS) * qn_ref[...]
        qt_ref[hh] = (qh * scale).T.astype(BF16)
        kh = qkv[:, D_ATTN + hh * HEAD_DIM:D_ATTN + (hh + 1) * HEAD_DIM]
        kh = kh * lax.rsqrt(jnp.mean(kh * kh, axis=-1, keepdims=True) + NORM_EPS) * kn_ref[...]
        kaug_ref[hh, :, 0:HEAD_DIM] = kh.astype(BF16)
        kaug_ref[hh, :, HEAD_DIM:AUG_DIM] = extras[:, sl].astype(BF16)
        vh = qkv[:, 2 * D_ATTN + hh * HEAD_DIM:2 * D_ATTN + (hh + 1) * HEAD_DIM]
        vt_ref[hh, 0:HEAD_DIM, :] = vh.T.astype(BF16)
        vt_ref[hh, HEAD_DIM:VT_ROWS, :] = jnp.ones((VT_ROWS - HEAD_DIM, tm), BF16)


def _attn_proj(x, layer, g, wqkv, bqkv, wft, bf, qn, kn, sel):
    bsz, seq, _ = x.shape
    tm = T_ATTN
    n_t = seq // tm
    out_shape = (
        jax.ShapeDtypeStruct((bsz, N_HEADS, n_t, HEAD_DIM, tm), BF16),
        jax.ShapeDtypeStruct((bsz, N_HEADS, seq, AUG_DIM), BF16),
        jax.ShapeDtypeStruct((bsz, N_HEADS, n_t, VT_ROWS, tm), BF16),
        jax.ShapeDtypeStruct((bsz, N_HEADS, seq), F32),
    )
    out_specs = (
        pl.BlockSpec((None, N_HEADS, None, HEAD_DIM, tm), lambda b, t: (b, 0, t, 0, 0)),
        pl.BlockSpec((None, N_HEADS, tm, AUG_DIM), lambda b, t: (b, 0, t, 0)),
        pl.BlockSpec((None, N_HEADS, None, VT_ROWS, tm), lambda b, t: (b, 0, t, 0, 0)),
        pl.BlockSpec((None, N_HEADS, tm), lambda b, t: (b, 0, t)),
    )
    stacked = (g, wqkv, bqkv, wft, bf, qn, kn)
    consts = stacked + (sel,)
    return pl.pallas_call(
        _attn_proj_kernel,
        grid=(bsz, n_t),
        in_specs=[pl.BlockSpec((None, tm, D_MODEL), lambda b, t: (b, t, 0))]
        + [_param_spec(a, layer) for a in stacked] + [_param_spec(sel)],
        out_specs=out_specs,
        out_shape=out_shape,
        scratch_shapes=[pltpu.VMEM((N_HEADS, V7X_LANES), F32)],
        compiler_params=pltpu.CompilerParams(
            dimension_semantics=("arbitrary", "arbitrary"), vmem_limit_bytes=_vmem_limit(VMEM_TILE_CALLS_MIB)),
        name="attn_proj",
    )(x, *consts)


def _gelu_tanh(x):
    return 0.5 * x * (1.0 + jnp.tanh(0.7978845608028654 * (x + 0.044715 * (x * x * x))))


def _sigmoid(x):
    return 0.5 * jnp.tanh(0.5 * x) + 0.5


def _rnn_kernel(x_ref, g_ref, perm_ref, unperm_ref, wxg_ref, bxg_ref, wm_ref, bm_ref, cw_ref,
                cb_ref, wgate_ref, ba_ref, bx_ref, lam_ref, wo_ref, sga_ref, rnn_ref, tail, h_carry):
    tm = x_ref.shape[0]
    sub = V7X_SUBLANES
    gw = V7X_MXU_DIM
    ngrp = D_RNN // gw
    nv = tm // sub
    ntail = CONV_WIDTH - 1
    mw = 2 * D_MODEL // ngrp

    @pl.when(pl.program_id(1) == 0)
    def _():
        tail[...] = jnp.zeros_like(tail)
        h_carry[...] = jnp.zeros_like(h_carry)

    sub_id = lax.broadcasted_iota(jnp.int32, (sub, gw), 0)
    first = sub_id == 0

    def vreg(arr, r):
        return arr[r * sub:(r + 1) * sub, :]

    h = _rms_norm(x_ref[...], g_ref[...]).astype(BF16)
    hp = jnp.dot(perm_ref[...], h, preferred_element_type=F32).astype(BF16)

    def xg_proj(j):
        return jnp.dot(hp, wxg_ref[j], preferred_element_type=F32) + bxg_ref[j]

    def merge_gate(j):
        lo, hi = j * mw, (j + 1) * mw
        return _sigmoid(jnp.dot(h, wm_ref[:, lo:hi], preferred_element_type=F32) + bm_ref[:, lo:hi])

    def conv_stage(j, xr):
        lo, hi = j * gw, (j + 1) * gw
        conv = cb_ref[:, lo:hi]
        for tap in range(CONV_WIDTH):
            k = CONV_WIDTH - 1 - tap
            if k == 0:
                shifted = xr
            else:
                head = []
                for i in range(k):
                    prev = pltpu.roll(tail[(ntail - k + i) * sub:(ntail - k + i + 1) * sub, lo:hi],
                                      1, axis=0)
                    cur = pltpu.roll(vreg(xr, nv - k + i), 1, axis=0)
                    head.append(jnp.where(first, prev, cur))
                shifted = jnp.concatenate(head + [xr[:(nv - k) * sub, :]], axis=0)
            conv = conv + shifted * cw_ref[tap:tap + 1, lo:hi]
        tail[:, lo:hi] = xr[(nv - ntail) * sub:, :]
        return conv

    def lru_stage(j, gz, conv, gr):
        lo, hi = j * gw, (j + 1) * gw
        r_gate = _sigmoid(gz[:, :gw] + ba_ref[:, lo:hi])
        i_gate = _sigmoid(gz[:, gw:] + bx_ref[:, lo:hi])
        nlam = -lam_ref[:, lo:hi]
        softplus = jnp.maximum(nlam, 0.0) + jnp.log1p(jnp.exp(-jnp.abs(nlam)))
        log_a = (-LRU_C * softplus) * r_gate
        a = jnp.exp(log_a)
        om = -jnp.tanh(log_a) * (a * a + 1.0)
        u = jnp.where(om == 0.0, 0.0, om * lax.rsqrt(om)) * (i_gate * conv)

        h_loc, a_cum = [vreg(u, 0)], [vreg(a, 0)]
        for r in range(1, nv):
            ar = vreg(a, r)
            h_loc.append(ar * h_loc[-1] + vreg(u, r))
            a_cum.append(ar * a_cum[-1])
        aa, bb = a_cum[-1], h_loc[-1]
        d = 1
        while d < sub:
            keep = sub_id >= d
            a_sh = jnp.where(keep, pltpu.roll(aa, d, axis=0), 1.0)
            b_sh = jnp.where(keep, pltpu.roll(bb, d, axis=0), 0.0)
            bb = aa * b_sh + bb
            aa = aa * a_sh
            d *= 2
        h_in = h_carry[:, lo:hi]
        chunk_end = aa * h_in + bb
        entry = jnp.where(first, h_in, pltpu.roll(chunk_end, 1, axis=0))
        h_carry[:, lo:hi] = jnp.broadcast_to(chunk_end[sub - 1:sub, :], (sub, gw))
        hs = jnp.concatenate([h_loc[r] + a_cum[r] * entry for r in range(nv)], axis=0)
        return (hs * _gelu_tanh(gr)).astype(BF16)

    def out_stage(j, yr):
        yr = jnp.dot(unperm_ref[...], yr, preferred_element_type=F32).astype(BF16)
        return jnp.dot(yr, wo_ref[j * gw:(j + 1) * gw, :], preferred_element_type=F32)

    def gate_stage(j, conv):
        return jnp.dot(conv.astype(BF16), wgate_ref[j], preferred_element_type=F32)

    mg = {0: merge_gate(0)}
    rest = {0: xg_proj(0), 1: xg_proj(1)}
    conv = {0: conv_stage(0, rest[0][:, :gw])}
    gz = {0: gate_stage(0, conv[0])}
    mg[1] = merge_gate(1)
    rest[2] = xg_proj(2)
    conv[1] = conv_stage(1, rest[1][:, :gw])
    yr = {0: lru_stage(0, gz[0], conv[0], rest[0][:, gw:])}
    gz[1] = gate_stage(1, conv[1])
    y_rnn = out_stage(0, yr[0])
    rest[3] = xg_proj(3)
    conv[2] = conv_stage(2, rest[2][:, :gw])
    yr[1] = lru_stage(1, gz[1], conv[1], rest[1][:, gw:])
    gz[2] = gate_stage(2, conv[2])
    y_rnn = y_rnn + out_stage(1, yr[1])
    conv[3] = conv_stage(3, rest[3][:, :gw])
    yr[2] = lru_stage(2, gz[2], conv[2], rest[2][:, gw:])
    gz[3] = gate_stage(3, conv[3])
    mg[2] = merge_gate(2)
    y_rnn = y_rnn + out_stage(2, yr[2])
    yr[3] = lru_stage(3, gz[3], conv[3], rest[3][:, gw:])
    mg[3] = merge_gate(3)
    y_rnn = y_rnn + out_stage(3, yr[3])

    sga_ref[...] = jnp.concatenate([mg[0], mg[1]], axis=1)
    rnn_ref[...] = jnp.concatenate([mg[2], mg[3]], axis=1) * y_rnn


def _time_permutation(tm):
    rho = jnp.arange(tm)
    src = (tm // V7X_SUBLANES) * (rho % V7X_SUBLANES) + rho // V7X_SUBLANES
    return (src[:, None] == jnp.arange(tm)[None, :]).astype(BF16)


def _rnn_branch(x, layer, g, wxg, bxg, wm, bm, cw, cb, wgate, ba, bx, lam, wo):
    bsz, seq, _ = x.shape
    tm = TM_RNN
    tile = pl.BlockSpec((None, tm, D_MODEL), lambda b, t: (b, t, 0))
    perm = _time_permutation(tm)
    shared = (perm, perm.T)
    stacked = (wxg, bxg, wm, bm, cw, cb, wgate, ba, bx, lam, wo)
    consts = (g,) + shared + stacked
    out = jax.ShapeDtypeStruct((bsz, seq, D_MODEL), F32)
    return pl.pallas_call(
        _rnn_kernel,
        grid=(bsz, seq // tm),
        in_specs=[tile, _param_spec(g, layer)] + [_param_spec(a) for a in shared]
        + [_param_spec(a, layer) for a in stacked],
        out_specs=(tile, tile),
        out_shape=(out, out),
        scratch_shapes=[
            pltpu.VMEM(((CONV_WIDTH - 1) * V7X_SUBLANES, D_RNN), F32),
            pltpu.VMEM((V7X_SUBLANES, D_RNN), F32),
        ],
        compiler_params=pltpu.CompilerParams(
            dimension_semantics=("arbitrary", "arbitrary"), vmem_limit_bytes=_vmem_limit(VMEM_TILE_CALLS_MIB)),
        name="rnn_branch",
    )(x, *consts)


def _fox_attn_kernel(qt_ref, c_ref, k_ref, vt_ref, o_ref, qt_aug, st0, st1, p0, p1, al0, al1,
                     bm0, bm1, m_s, acc):
    tq = o_ref.shape[0]
    tk = vt_ref.shape[2]
    hh = pl.program_id(1)
    qi = pl.program_id(2)
    st, pb, al, bm = (st0, st1), (p0, p1), (al0, al1), (bm0, bm1)

    for ci in range(qt_ref.shape[0]):
        qt_aug[0:HEAD_DIM, ci * tk:(ci + 1) * tk] = qt_ref[ci]
    hi, mid, lo = _split3(c_ref[pl.ds(hh, 1), :] * LOG2_E)
    row = lax.broadcasted_iota(jnp.int32, (V7X_SUBLANES, tq), 0)
    top = jnp.where(row == 0, hi, jnp.where(row == 1, mid, jnp.where(row == 2, lo,
          jnp.where(row < 2 * N_SPLIT, 1.0, 0.0))))
    qt_aug[HEAD_DIM:AUG_DIM, :] = jnp.concatenate(
        [top, jnp.zeros((V7X_LANES - V7X_SUBLANES, tq), F32)], axis=0).astype(BF16)

    m_s[...] = jnp.full(m_s.shape, MASK_VALUE, F32)
    acc[...] = jnp.zeros_like(acc)
    p1[...] = jnp.zeros_like(p1)
    al1[...] = jnp.ones_like(al1)

    def score(j, s, c0=0, with_max=True):
        kb = k_ref[pl.ds(pl.multiple_of(j * tk, tk), tk), :]
        blk = jnp.dot(kb, qt_aug[:, c0:], preferred_element_type=F32)
        st[s][:, c0:tq] = blk
        if with_max:
            bm[s][...] = jnp.max(blk, axis=0, keepdims=True)

    def softmax(s, c0=0, diagonal=False):
        st_ref = st[s]
        if diagonal:
            shape = (tk, tq - c0)
            visible = (lax.broadcasted_iota(jnp.int32, shape, 0)
                       <= lax.broadcasted_iota(jnp.int32, shape, 1))
            st_ref[:, c0:tq] = jnp.where(visible, st_ref[:, c0:tq], MASK_VALUE)
            blk_max = jnp.max(st_ref[:, c0:tq], axis=0, keepdims=True)
        else:
            blk_max = bm[s][...]
        m_old = m_s[:, c0:]
        m_new = jnp.maximum(m_old, blk_max)
        m_s[:, c0:] = m_new
        al[s][:, c0:] = jnp.exp2(m_old - m_new)
        pb[s][:, c0:tq] = jnp.exp2(st_ref[:, c0:tq] - m_new).astype(BF16)

    def pv(j, s, c0=0):
        acc[:, c0:tq] = al[s][:, c0:] * acc[:, c0:tq] + jnp.dot(
            vt_ref[j], pb[s][:, c0:tq], preferred_element_type=F32)

    def visible_blocks(a, nblk):
        score(a, 0)
        score(a + 1, 1)
        for d in range(nblk):
            s = d % 2
            softmax(s)
            pv(jnp.maximum(a + d - 1, 0), 1 - s)
            if d + 2 < nblk:
                score(a + d + 2, s)

    long_blk = LONG_TRIP_Q * KV_PER_Q
    n_long = qi // LONG_TRIP_Q

    def long_trip(t, carry):
        visible_blocks(long_blk * t, long_blk)
        return carry

    def short_trip(t, carry):
        visible_blocks(long_blk * n_long + KV_PER_Q * t, KV_PER_Q)
        return carry

    lax.fori_loop(0, n_long, long_trip, 0)
    lax.fori_loop(0, qi - LONG_TRIP_Q * n_long, short_trip, 0)
    a = KV_PER_Q * qi
    score(a, 0, with_max=False)
    score(a + 1, 1, c0=tk, with_max=False)
    for d in range(KV_PER_Q):
        s = d % 2
        softmax(s, c0=d * tk, diagonal=True)
        pv(jnp.maximum(a + d - 1, 0), 1 - s, c0=max(d - 1, 0) * tk)
        if d + 2 < KV_PER_Q:
            score(a + d + 2, s, c0=(d + 2) * tk, with_max=False)
    pv(a + KV_PER_Q - 1, (KV_PER_Q - 1) % 2, c0=(KV_PER_Q - 1) * tk)
    o_ref[...] = (acc[0:HEAD_DIM, 0:tq] / acc[HEAD_DIM:HEAD_DIM + 1, 0:tq]).T.astype(BF16)


def _fox_attn(qt, c, kaug, vt):
    bsz, seq = c.shape[0], c.shape[2]
    n_t, tk = vt.shape[2], vt.shape[4]
    tq = TQ_ATTN
    assert tq == KV_PER_Q * tk and KV_PER_Q % 2 == 0 and seq % tq == 0
    return pl.pallas_call(
        _fox_attn_kernel,
        grid=(bsz, N_HEADS, seq // tq),
        in_specs=[
            pl.BlockSpec((None, None, KV_PER_Q, HEAD_DIM, tk), lambda b, h, i: (b, h, i, 0, 0)),
            pl.BlockSpec((None, N_HEADS, tq), lambda b, h, i: (b, 0, i)),
            pl.BlockSpec((None, None, seq, AUG_DIM), lambda b, h, i: (b, h, 0, 0)),
            pl.BlockSpec((None, None, n_t, VT_ROWS, tk), lambda b, h, i: (b, h, 0, 0, 0)),
        ],
        out_specs=pl.BlockSpec((None, tq, HEAD_DIM), lambda b, h, i: (b, i, h)),
        out_shape=jax.ShapeDtypeStruct((bsz, seq, D_ATTN), BF16),
        scratch_shapes=[
            pltpu.VMEM((AUG_DIM, tq), BF16),
            pltpu.VMEM((tk, tq + SCRATCH_PAD), F32),
            pltpu.VMEM((tk, tq + SCRATCH_PAD), F32),
            pltpu.VMEM((tk, tq + SCRATCH_PAD), BF16),
            pltpu.VMEM((tk, tq + SCRATCH_PAD), BF16),
            pltpu.VMEM((1, tq), F32),
            pltpu.VMEM((1, tq), F32),
            pltpu.VMEM((1, tq), F32),
            pltpu.VMEM((1, tq), F32),
            pltpu.VMEM((1, tq), F32),
            pltpu.VMEM((VT_ROWS, tq + SCRATCH_PAD), F32),
        ],
        compiler_params=pltpu.CompilerParams(
            dimension_semantics=("arbitrary", "arbitrary", "arbitrary"),
            vmem_limit_bytes=_vmem_limit(VMEM_LARGE_CALLS_MIB)),
        name="fox_attn",
    )(qt, c, kaug, vt)


def _out_ffn_kernel(x_ref, attn_ref, sga_ref, rnn_ref, woa_ref, wout_ref, g_ref, wg_ref, wu_ref,
                    wd_ref, o_ref):
    y_attn = jnp.dot(attn_ref[...], woa_ref[...], preferred_element_type=F32)
    merged = sga_ref[...] * y_attn + rnn_ref[...]
    x1 = x_ref[...] + jnp.dot(merged.astype(BF16), wout_ref[...], preferred_element_type=F32)
    h = _rms_norm(x1, g_ref[...]).astype(BF16)
    o_ref[...] = x1 + 0.5 * _swiglu(h, wg_ref, wu_ref, wd_ref)


def _out_ffn(x2d, attn2d, sga2d, rnn2d, layer, woa, wout, g, wg, wu, wd):
    n_tok = x2d.shape[0]
    tm = TM_OUT
    tile = pl.BlockSpec((tm, D_MODEL), lambda i: (i, 0))
    consts = (woa, wout, g, wg, wu, wd)
    return pl.pallas_call(
        _out_ffn_kernel,
        grid=(n_tok // tm,),
        in_specs=[tile, tile, tile, tile] + [_param_spec(a, layer) for a in consts],
        out_specs=tile,
        out_shape=jax.ShapeDtypeStruct((n_tok, D_MODEL), F32),
        compiler_params=pltpu.CompilerParams(
            dimension_semantics=("parallel",), vmem_limit_bytes=_vmem_limit(VMEM_LARGE_CALLS_MIB)),
        name="out_ffn",
    )(x2d, attn2d, sga2d, rnn2d, *consts)


def _block_diag_tiles(w):
    per = V7X_MXU_DIM // RNN_BLOCK
    w4 = w.reshape(w.shape[0], N_RNN_BLOCKS // per, per, RNN_BLOCK, RNN_BLOCK)
    eye = jnp.eye(per, dtype=w.dtype)
    return jnp.einsum("ljarc,ab->ljarbc", w4, eye).reshape(
        w.shape[0], -1, V7X_MXU_DIM, V7X_MXU_DIM)


def _bias_selector():
    r = jnp.arange(V7X_LANES)[:, None]
    col = jnp.arange(N_HEADS * V7X_LANES)[None, :]
    head, pos = col // V7X_LANES, col % V7X_LANES
    ones_row = N_SPLIT * N_HEADS + head
    piece_row = (pos - N_SPLIT) * N_HEADS + head
    hit = jnp.where(pos < N_SPLIT, r == ones_row, (pos < 2 * N_SPLIT) & (r == piece_row))
    return hit.astype(BF16)


def kernel(x, ffn1_norm, ffn1_w_gate, ffn1_w_up, ffn1_w_down, mix_norm, w_in, b_in, q_norm, k_norm, conv_w, conv_b, lru_w_a, lru_b_a, lru_w_x, lru_b_x, lru_lambda, w_o_attn, w_o_rnn, w_out, ffn2_norm, ffn2_w_gate, ffn2_w_up, ffn2_w_down):
    bsz, seq, _ = x.shape
    depth = w_in.shape[0]
    n_tok = bsz * seq
    assert seq % T_ATTN == 0 and seq % TM_RNN == 0 and n_tok % TM_FFN == 0 and n_tok % TM_OUT == 0
    row = lambda v: v.reshape(depth, 1, -1).astype(F32)
    bf16 = lambda w: w.astype(BF16)
    sel = _bias_selector()
    f_lo, f_hi = 3 * D_ATTN, 3 * D_ATTN + N_HEADS
    ngrp = D_RNN // V7X_MXU_DIM

    ffn1 = (row(ffn1_norm), bf16(ffn1_w_gate), bf16(ffn1_w_up), bf16(ffn1_w_down))
    wft = jnp.zeros((depth, 2 * V7X_SUBLANES, D_MODEL), BF16).at[:, :N_HEADS].set(
        bf16(jnp.swapaxes(w_in[:, :, f_lo:f_hi], 1, 2)))
    attn_w = (row(mix_norm), bf16(w_in[:, :, :f_lo]), row(b_in[:, :f_lo]), wft,
              b_in[:, f_lo:f_hi].reshape(depth, N_HEADS, 1), row(q_norm), row(k_norm))
    wr, br = bf16(w_in[:, :, f_hi:]), b_in[:, f_hi:]
    wxg = wr[:, :, :2 * D_RNN].reshape(depth, D_MODEL, 2, ngrp, V7X_MXU_DIM).transpose(
        0, 3, 1, 2, 4).reshape(depth, ngrp, D_MODEL, 2 * V7X_MXU_DIM)
    bxg = br[:, :2 * D_RNN].reshape(depth, 2, ngrp, V7X_MXU_DIM).transpose(0, 2, 1, 3).reshape(
        depth, ngrp, 1, 2 * V7X_MXU_DIM)
    wgate = bf16(jnp.concatenate([_block_diag_tiles(lru_w_a), _block_diag_tiles(lru_w_x)], axis=3))
    rnn_w = (wxg, bxg, wr[:, :, 2 * D_RNN:], row(br[:, 2 * D_RNN:]), conv_w.astype(F32),
             row(conv_b), wgate, row(lru_b_a), row(lru_b_x), row(lru_lambda), bf16(w_o_rnn))
    out_w = (bf16(w_o_attn), bf16(w_out), row(ffn2_norm), bf16(ffn2_w_gate), bf16(ffn2_w_up),
             bf16(ffn2_w_down))

    for l in range(depth):
        x = _ffn(x.reshape(n_tok, D_MODEL), l, *ffn1).reshape(bsz, seq, D_MODEL)
        qt, kaug, vt, c = _attn_proj(x, l, *attn_w, sel)
        sga, rnn = _rnn_branch(x, l, attn_w[0], *rnn_w)
        attn = _fox_attn(qt, c, kaug, vt)
        x = _out_ffn(
            x.reshape(n_tok, D_MODEL), attn.reshape(n_tok, D_ATTN), sga.reshape(n_tok, D_MODEL),
            rnn.reshape(n_tok, D_MODEL), l, *out_w).reshape(bsz, seq, D_MODEL)
    return x
```

```python
import jax
import jax.numpy as jnp
from jax import lax
from jax.experimental import pallas as pl
from jax.experimental.pallas import tpu as pltpu

F32 = jnp.float32
BF16 = jnp.bfloat16

D_MODEL = 1024
N_HEADS = 8
HEAD_DIM = 128
D_ATTN = N_HEADS * HEAD_DIM
D_RNN = D_MODEL
N_RNN_BLOCKS = 16
RNN_BLOCK = D_RNN // N_RNN_BLOCKS
CONV_WIDTH = 4
LRU_C = 8.0
D_FF = 2816
NORM_EPS = 1e-6

V7X_LANES = 128
V7X_SUBLANES = 8
V7X_MXU_DIM = 256
V7X_VMEM_BYTES = 64 * 1024 * 1024

AUG_DIM = HEAD_DIM + V7X_LANES
N_SPLIT = 3

FF_CHUNKS = ((0, 1024), (1024, 2048), (2048, D_FF))

TM_FFN = 1024
TM_OUT = 512
TM_RNN = 512
T_ATTN = 512
KV_PER_Q = 4
TQ_ATTN = KV_PER_Q * T_ATTN
LONG_TRIP_Q = 2
SCRATCH_PAD = 2 * V7X_LANES
VT_ROWS = HEAD_DIM + 16

MASK_VALUE = -1e30
LOG2_E = 1.4426950408889634


MIB = 1024 * 1024
VMEM_TILE_CALLS_MIB = 48
VMEM_LARGE_CALLS_MIB = 56


def _vmem_limit(mib):
    assert mib * MIB < V7X_VMEM_BYTES
    return mib * MIB


def _param_spec(arr, layer=None):
    if layer is None:
        block, idx = arr.shape, (0,) * arr.ndim
    else:
        block, idx = (None,) + arr.shape[1:], (layer,) + (0,) * (arr.ndim - 1)
    return pl.BlockSpec(block, lambda *_: idx, pipeline_mode=pl.Buffered(1))


def _rms_norm(x, g):
    return x * lax.rsqrt(jnp.mean(x * x, axis=-1, keepdims=True) + NORM_EPS) * g


def _swiglu(h_bf, wg_ref, wu_ref, wd_ref):
    acc = None
    for lo, hi in FF_CHUNKS:
        g = jnp.dot(h_bf, wg_ref[:, lo:hi], preferred_element_type=F32)
        u = jnp.dot(h_bf, wu_ref[:, lo:hi], preferred_element_type=F32)
        a = (g * jax.nn.sigmoid(g) * u).astype(BF16)
        part = jnp.dot(a, wd_ref[lo:hi, :], preferred_element_type=F32)
        acc = part if acc is None else acc + part
    return acc


def _ffn_kernel(x_ref, g_ref, wg_ref, wu_ref, wd_ref, o_ref):
    x = x_ref[...]
    h = _rms_norm(x, g_ref[...]).astype(BF16)
    o_ref[...] = x + 0.5 * _swiglu(h, wg_ref, wu_ref, wd_ref)


def _ffn(x2d, layer, g, wg, wu, wd):
    n_tok = x2d.shape[0]
    tm = TM_FFN
    tile = pl.BlockSpec((tm, D_MODEL), lambda i: (i, 0))
    return pl.pallas_call(
        _ffn_kernel,
        grid=(n_tok // tm,),
        in_specs=[tile] + [_param_spec(a, layer) for a in (g, wg, wu, wd)],
        out_specs=tile,
        out_shape=jax.ShapeDtypeStruct((n_tok, D_MODEL), F32),
        compiler_params=pltpu.CompilerParams(
            dimension_semantics=("parallel",), vmem_limit_bytes=_vmem_limit(VMEM_TILE_CALLS_MIB)),
        name="ffn",
    )(x2d, g, wg, wu, wd)


def _split3(c):
    hi = c.astype(BF16).astype(F32)
    r = c - hi
    mid = r.astype(BF16).astype(F32)
    lo = (r - mid).astype(BF16).astype(F32)
    return hi, mid, lo


def _attn_proj_kernel(x_ref, g_ref, wqkv_ref, bqkv_ref, wft_ref, bf_ref, qn_ref, kn_ref, sel_ref,
                      qt_ref, kaug_ref, vt_ref, c_ref, c_carry):
    tm = x_ref.shape[0]

    @pl.when(pl.program_id(1) == 0)
    def _():
        c_carry[...] = jnp.zeros_like(c_carry)

    h = _rms_norm(x_ref[...], g_ref[...]).astype(BF16)

    fl = lax.dot_general(wft_ref[...], h, (((1,), (1,)), ((), ())), preferred_element_type=F32)
    fl = fl[:N_HEADS, :] + bf_ref[...]
    log_f = jnp.minimum(fl, 0.0) - jnp.log1p(jnp.exp(-jnp.abs(fl)))

    lane = lax.broadcasted_iota(jnp.int32, log_f.shape, 1)
    cs = log_f
    d = 1
    while d < tm:
        cs = cs + jnp.where(lane >= d, pltpu.roll(cs, d, axis=1), 0.0)
        d *= 2
    c = cs + c_carry[:, 0:1]
    c_carry[...] = jnp.broadcast_to(c[:, tm - 1:tm], c_carry.shape)
    c_ref[...] = c

    hi, mid, lo = _split3(c * LOG2_E)
    ones = jnp.ones_like(c)
    pad = jnp.zeros((V7X_LANES - (N_SPLIT + 1) * N_HEADS, tm), F32)
    stack = jnp.concatenate([-hi, -mid, -lo, ones, pad], axis=0)
    stack_t = stack.T.astype(BF16)

    qkv = jnp.dot(h, wqkv_ref[...], preferred_element_type=F32) + bqkv_ref[...]
    extras = jnp.dot(stack_t, sel_ref[...], preferred_element_type=F32)

    scale = HEAD_DIM ** -0.5 * LOG2_E
    for hh in range(N_HEADS):
        sl = slice(hh * HEAD_DIM, (hh + 1) * HEAD_DIM)
        qh = qkv[:, sl]
        qh = qh * lax.rsqrt(jnp.mean(qh * qh, axis=-1, keepdims=True) + NORM_EPS) * qn_ref[...]
        qt_ref[hh] = (qh * scale).T.astype(BF16)
        kh = qkv[:, D_ATTN + hh * HEAD_DIM:D_ATTN + (hh + 1) * HEAD_DIM]
        kh = kh * lax.rsqrt(jnp.mean(kh * kh, axis=-1, keepdims=True) + NORM_EPS) * kn_ref[...]
        kaug_ref[hh, :, 0:HEAD_DIM] = kh.astype(BF16)
        kaug_ref[hh, :, HEAD_DIM:AUG_DIM] = extras[:, sl].astype(BF16)
        vh = qkv[:, 2 * D_ATTN + hh * HEAD_DIM:2 * D_ATTN + (hh + 1) * HEAD_DIM]
        vt_ref[hh, 0:HEAD_DIM, :] = vh.T.astype(BF16)
        vt_ref[hh, HEAD_DIM:VT_ROWS, :] = jnp.ones((VT_ROWS - HEAD_DIM, tm), BF16)


def _attn_proj(x, layer, g, wqkv, bqkv, wft, bf, qn, kn, sel):
    bsz, seq, _ = x.shape
    tm = T_ATTN
    n_t = seq // tm
    out_shape = (
        jax.ShapeDtypeStruct((bsz, N_HEADS, n_t, HEAD_DIM, tm), BF16),
        jax.ShapeDtypeStruct((bsz, N_HEADS, seq, AUG_DIM), BF16),
        jax.ShapeDtypeStruct((bsz, N_HEADS, n_t, VT_ROWS, tm), BF16),
        jax.ShapeDtypeStruct((bsz, N_HEADS, seq), F32),
    )
    out_specs = (
        pl.BlockSpec((None, N_HEADS, None, HEAD_DIM, tm), lambda b, t: (b, 0, t, 0, 0)),
        pl.BlockSpec((None, N_HEADS, tm, AUG_DIM), lambda b, t: (b, 0, t, 0)),
        pl.BlockSpec((None, N_HEADS, None, VT_ROWS, tm), lambda b, t: (b, 0, t, 0, 0)),
        pl.BlockSpec((None, N_HEADS, tm), lambda b, t: (b, 0, t)),
    )
    stacked = (g, wqkv, bqkv, wft, bf, qn, kn)
    consts = stacked + (sel,)
    return pl.pallas_call(
        _attn_proj_kernel,
        grid=(bsz, n_t),
        in_specs=[pl.BlockSpec((None, tm, D_MODEL), lambda b, t: (b, t, 0))]
        + [_param_spec(a, layer) for a in stacked] + [_param_spec(sel)],
        out_specs=out_specs,
        out_shape=out_shape,
        scratch_shapes=[pltpu.VMEM((N_HEADS, V7X_LANES), F32)],
        compiler_params=pltpu.CompilerParams(
            dimension_semantics=("arbitrary", "arbitrary"), vmem_limit_bytes=_vmem_limit(VMEM_TILE_CALLS_MIB)),
        name="attn_proj",
    )(x, *consts)


def _gelu_tanh(x):
    return 0.5 * x * (1.0 + jnp.tanh(0.7978845608028654 * (x + 0.044715 * (x * x * x))))


def _sigmoid(x):
    return 0.5 * jnp.tanh(0.5 * x) + 0.5


def _rnn_kernel(x_ref, g_ref, perm_ref, unperm_ref, wxg_ref, bxg_ref, wm_ref, bm_ref, cw_ref,
                cb_ref, wgate_ref, ba_ref, bx_ref, lam_ref, wo_ref, sga_ref, rnn_ref, tail, h_carry):
    tm = x_ref.shape[0]
    sub = V7X_SUBLANES
    gw = V7X_MXU_DIM
    ngrp = D_RNN // gw
    nv = tm // sub
    ntail = CONV_WIDTH - 1
    mw = 2 * D_MODEL // ngrp

    @pl.when(pl.program_id(1) == 0)
    def _():
        tail[...] = jnp.zeros_like(tail)
        h_carry[...] = jnp.zeros_like(h_carry)

    sub_id = lax.broadcasted_iota(jnp.int32, (sub, gw), 0)
    first = sub_id == 0

    def vreg(arr, r):
        return arr[r * sub:(r + 1) * sub, :]

    h = _rms_norm(x_ref[...], g_ref[...]).astype(BF16)
    hp = jnp.dot(perm_ref[...], h, preferred_element_type=F32).astype(BF16)

    def xg_proj(j):
        return jnp.dot(hp, wxg_ref[j], preferred_element_type=F32) + bxg_ref[j]

    def merge_gate(j):
        lo, hi = j * mw, (j + 1) * mw
        return _sigmoid(jnp.dot(h, wm_ref[:, lo:hi], preferred_element_type=F32) + bm_ref[:, lo:hi])

    def conv_stage(j, xr):
        lo, hi = j * gw, (j + 1) * gw
        conv = cb_ref[:, lo:hi]
        for tap in range(CONV_WIDTH):
            k = CONV_WIDTH - 1 - tap
            if k == 0:
                shifted = xr
            else:
                head = []
                for i in range(k):
                    prev = pltpu.roll(tail[(ntail - k + i) * sub:(ntail - k + i + 1) * sub, lo:hi],
                                      1, axis=0)
                    cur = pltpu.roll(vreg(xr, nv - k + i), 1, axis=0)
                    head.append(jnp.where(first, prev, cur))
                shifted = jnp.concatenate(head + [xr[:(nv - k) * sub, :]], axis=0)
            conv = conv + shifted * cw_ref[tap:tap + 1, lo:hi]
        tail[:, lo:hi] = xr[(nv - ntail) * sub:, :]
        return conv

    def lru_stage(j, gz, conv, gr):
        lo, hi = j * gw, (j + 1) * gw
        r_gate = _sigmoid(gz[:, :gw] + ba_ref[:, lo:hi])
        i_gate = _sigmoid(gz[:, gw:] + bx_ref[:, lo:hi])
        nlam = -lam_ref[:, lo:hi]
        softplus = jnp.maximum(nlam, 0.0) + jnp.log1p(jnp.exp(-jnp.abs(nlam)))
        log_a = (-LRU_C * softplus) * r_gate
        a = jnp.exp(log_a)
        om = -jnp.tanh(log_a) * (a * a + 1.0)
        u = jnp.where(om == 0.0, 0.0, om * lax.rsqrt(om)) * (i_gate * conv)

        h_loc, a_cum = [vreg(u, 0)], [vreg(a, 0)]
        for r in range(1, nv):
            ar = vreg(a, r)
            h_loc.append(ar * h_loc[-1] + vreg(u, r))
            a_cum.append(ar * a_cum[-1])
        aa, bb = a_cum[-1], h_loc[-1]
        d = 1
        while d < sub:
            keep = sub_id >= d
            a_sh = jnp.where(keep, pltpu.roll(aa, d, axis=0), 1.0)
            b_sh = jnp.where(keep, pltpu.roll(bb, d, axis=0), 0.0)
            bb = aa * b_sh + bb
            aa = aa * a_sh
            d *= 2
        h_in = h_carry[:, lo:hi]
        chunk_end = aa * h_in + bb
        entry = jnp.where(first, h_in, pltpu.roll(chunk_end, 1, axis=0))
        h_carry[:, lo:hi] = jnp.broadcast_to(chunk_end[sub - 1:sub, :], (sub, gw))
        hs = jnp.concatenate([h_loc[r] + a_cum[r] * entry for r in range(nv)], axis=0)
        return (hs * _gelu_tanh(gr)).astype(BF16)

    def out_stage(j, yr):
        yr = jnp.dot(unperm_ref[...], yr, preferred_element_type=F32).astype(BF16)
        return jnp.dot(yr, wo_ref[j * gw:(j + 1) * gw, :], preferred_element_type=F32)

    def gate_stage(j, conv):
        return jnp.dot(conv.astype(BF16), wgate_ref[j], preferred_element_type=F32)

    mg = {0: merge_gate(0)}
    rest = {0: xg_proj(0), 1: xg_proj(1)}
    conv = {0: conv_stage(0, rest[0][:, :gw])}
    gz = {0: gate_stage(0, conv[0])}
    mg[1] = merge_gate(1)
    rest[2] = xg_proj(2)
    conv[1] = conv_stage(1, rest[1][:, :gw])
    yr = {0: lru_stage(0, gz[0], conv[0], rest[0][:, gw:])}
    gz[1] = gate_stage(1, conv[1])
    y_rnn = out_stage(0, yr[0])
    rest[3] = xg_proj(3)
    conv[2] = conv_stage(2, rest[2][:, :gw])
    yr[1] = lru_stage(1, gz[1], conv[1], rest[1][:, gw:])
    gz[2] = gate_stage(2, conv[2])
    y_rnn = y_rnn + out_stage(1, yr[1])
    conv[3] = conv_stage(3, rest[3][:, :gw])
    yr[2] = lru_stage(2, gz[2], conv[2], rest[2][:, gw:])
    gz[3] = gate_stage(3, conv[3])
    mg[2] = merge_gate(2)
    y_rnn = y_rnn + out_stage(2, yr[2])
    yr[3] = lru_stage(3, gz[3], conv[3], rest[3][:, gw:])
    mg[3] = merge_gate(3)
    y_rnn = y_rnn + out_stage(3, yr[3])

    sga_ref[...] = jnp.concatenate([mg[0], mg[1]], axis=1)
    rnn_ref[...] = jnp.concatenate([mg[2], mg[3]], axis=1) * y_rnn


def _time_permutation(tm):
    rho = jnp.arange(tm)
    src = (tm // V7X_SUBLANES) * (rho % V7X_SUBLANES) + rho // V7X_SUBLANES
    return (src[:, None] == jnp.arange(tm)[None, :]).astype(BF16)


def _rnn_branch(x, layer, g, wxg, bxg, wm, bm, cw, cb, wgate, ba, bx, lam, wo):
    bsz, seq, _ = x.shape
    tm = TM_RNN
    tile = pl.BlockSpec((None, tm, D_MODEL), lambda b, t: (b, t, 0))
    perm = _time_permutation(tm)
    shared = (perm, perm.T)
    stacked = (wxg, bxg, wm, bm, cw, cb, wgate, ba, bx, lam, wo)
    consts = (g,) + shared + stacked
    out = jax.ShapeDtypeStruct((bsz, seq, D_MODEL), F32)
    return pl.pallas_call(
        _rnn_kernel,
        grid=(bsz, seq // tm),
        in_specs=[tile, _param_spec(g, layer)] + [_param_spec(a) for a in shared]
        + [_param_spec(a, layer) for a in stacked],
        out_specs=(tile, tile),
        out_shape=(out, out),
        scratch_shapes=[
            pltpu.VMEM(((CONV_WIDTH - 1) * V7X_SUBLANES, D_RNN), F32),
            pltpu.VMEM((V7X_SUBLANES, D_RNN), F32),
        ],
        compiler_params=pltpu.CompilerParams(
            dimension_semantics=("arbitrary", "arbitrary"), vmem_limit_bytes=_vmem_limit(VMEM_TILE_CALLS_MIB)),
        name="rnn_branch",
    )(x, *consts)


def _fox_attn_kernel(qt_ref, c_ref, k_ref, vt_ref, o_ref, qt_aug, st0, st1, p0, p1, al0, al1,
                     bm0, bm1, m_s, acc):
    tq = o_ref.shape[0]
    tk = vt_ref.shape[2]
    hh = pl.program_id(1)
    qi = pl.program_id(2)
    st, pb, al, bm = (st0, st1), (p0, p1), (al0, al1), (bm0, bm1)

    for ci in range(qt_ref.shape[0]):
        qt_aug[0:HEAD_DIM, ci * tk:(ci + 1) * tk] = qt_ref[ci]
    hi, mid, lo = _split3(c_ref[pl.ds(hh, 1), :] * LOG2_E)
    row = lax.broadcasted_iota(jnp.int32, (V7X_SUBLANES, tq), 0)
    top = jnp.where(row == 0, hi, jnp.where(row == 1, mid, jnp.where(row == 2, lo,
          jnp.where(row < 2 * N_SPLIT, 1.0, 0.0))))
    qt_aug[HEAD_DIM:AUG_DIM, :] = jnp.concatenate(
        [top, jnp.zeros((V7X_LANES - V7X_SUBLANES, tq), F32)], axis=0).astype(BF16)

    m_s[...] = jnp.full(m_s.shape, MASK_VALUE, F32)
    acc[...] = jnp.zeros_like(acc)

    @pl.when(qi == 0)
    def _():
        p1[...] = jnp.zeros_like(p1)
        al1[...] = jnp.ones_like(al1)

    def score(j, s, c0=0, with_max=True):
        kb = k_ref[pl.ds(pl.multiple_of(j * tk, tk), tk), :]
        blk = jnp.dot(kb, qt_aug[:, c0:], preferred_element_type=F32)
        st[s][:, c0:tq] = blk
        if with_max:
            bm[s][...] = jnp.max(blk, axis=0, keepdims=True)

    def softmax(s, c0=0, diagonal=False):
        st_ref = st[s]
        if diagonal:
            shape = (tk, tq - c0)
            visible = (lax.broadcasted_iota(jnp.int32, shape, 0)
                       <= lax.broadcasted_iota(jnp.int32, shape, 1))
            st_ref[:, c0:tq] = jnp.where(visible, st_ref[:, c0:tq], MASK_VALUE)
            blk_max = jnp.max(st_ref[:, c0:tq], axis=0, keepdims=True)
        else:
            blk_max = bm[s][...]
        m_old = m_s[:, c0:]
        m_new = jnp.maximum(m_old, blk_max)
        m_s[:, c0:] = m_new
        al[s][:, c0:] = jnp.exp2(m_old - m_new)
        pb[s][:, c0:tq] = jnp.exp2(st_ref[:, c0:tq] - m_new).astype(BF16)

    def pv(j, s, c0=0):
        acc[:, c0:tq] = al[s][:, c0:] * acc[:, c0:tq] + jnp.dot(
            vt_ref[j], pb[s][:, c0:tq], preferred_element_type=F32)

    def visible_blocks(a, nblk, first=False):
        score(a, 0)
        score(a + 1, 1)
        for d in range(nblk):
            s = d % 2
            softmax(s)
            if not (first and d == 0):
                pv(a + d - 1, 1 - s)
            if d + 2 < nblk:
                score(a + d + 2, s)

    @pl.when(qi > 0)
    def _():
        visible_blocks(jnp.int32(0), KV_PER_Q, first=True)

    long_blk = LONG_TRIP_Q * KV_PER_Q
    rest_q = jnp.maximum(qi - 1, 0)
    n_long = rest_q // LONG_TRIP_Q

    def long_trip(t, carry):
        visible_blocks(KV_PER_Q + long_blk * t, long_blk)
        return carry

    def short_trip(t, carry):
        visible_blocks(KV_PER_Q + long_blk * n_long + KV_PER_Q * t, KV_PER_Q)
        return carry

    lax.fori_loop(0, n_long, long_trip, 0)
    lax.fori_loop(0, rest_q - LONG_TRIP_Q * n_long, short_trip, 0)
    a = KV_PER_Q * qi
    score(a, 0, with_max=False)
    score(a + 1, 1, c0=tk, with_max=False)
    for d in range(KV_PER_Q):
        s = d % 2
        softmax(s, c0=d * tk, diagonal=True)
        pv(jnp.maximum(a + d - 1, 0), 1 - s, c0=max(d - 1, 0) * tk)
        if d + 2 < KV_PER_Q:
            score(a + d + 2, s, c0=(d + 2) * tk, with_max=False)
    pv(a + KV_PER_Q - 1, (KV_PER_Q - 1) % 2, c0=(KV_PER_Q - 1) * tk)
    o_ref[...] = (acc[0:HEAD_DIM, 0:tq] / acc[HEAD_DIM:HEAD_DIM + 1, 0:tq]).T.astype(BF16)


def _fox_attn(qt, c, kaug, vt):
    bsz, seq = c.shape[0], c.shape[2]
    n_t, tk = vt.shape[2], vt.shape[4]
    tq = TQ_ATTN
    assert tq == KV_PER_Q * tk and KV_PER_Q % 2 == 0 and seq % tq == 0
    return pl.pallas_call(
        _fox_attn_kernel,
        grid=(bsz, N_HEADS, seq // tq),
        in_specs=[
            pl.BlockSpec((None, None, KV_PER_Q, HEAD_DIM, tk), lambda b, h, i: (b, h, i, 0, 0)),
            pl.BlockSpec((None, N_HEADS, tq), lambda b, h, i: (b, 0, i)),
            pl.BlockSpec((None, None, seq, AUG_DIM), lambda b, h, i: (b, h, 0, 0)),
            pl.BlockSpec((None, None, n_t, VT_ROWS, tk), lambda b, h, i: (b, h, 0, 0, 0)),
        ],
        out_specs=pl.BlockSpec((None, tq, HEAD_DIM), lambda b, h, i: (b, i, h)),
        out_shape=jax.ShapeDtypeStruct((bsz, seq, D_ATTN), BF16),
        scratch_shapes=[
            pltpu.VMEM((AUG_DIM, tq), BF16),
            pltpu.VMEM((tk, tq + SCRATCH_PAD), F32),
            pltpu.VMEM((tk, tq + SCRATCH_PAD), F32),
            pltpu.VMEM((tk, tq + SCRATCH_PAD), BF16),
            pltpu.VMEM((tk, tq + SCRATCH_PAD), BF16),
            pltpu.VMEM((1, tq), F32),
            pltpu.VMEM((1, tq), F32),
            pltpu.VMEM((1, tq), F32),
            pltpu.VMEM((1, tq), F32),
            pltpu.VMEM((1, tq), F32),
            pltpu.VMEM((VT_ROWS, tq + SCRATCH_PAD), F32),
        ],
        compiler_params=pltpu.CompilerParams(
            dimension_semantics=("arbitrary", "arbitrary", "arbitrary"),
            vmem_limit_bytes=_vmem_limit(VMEM_LARGE_CALLS_MIB)),
        name="fox_attn",
    )(qt, c, kaug, vt)


def _out_ffn_kernel(x_ref, attn_ref, sga_ref, rnn_ref, woa_ref, wout_ref, g_ref, wg_ref, wu_ref,
                    wd_ref, o_ref):
    y_attn = jnp.dot(attn_ref[...], woa_ref[...], preferred_element_type=F32)
    merged = sga_ref[...] * y_attn + rnn_ref[...]
    x1 = x_ref[...] + jnp.dot(merged.astype(BF16), wout_ref[...], preferred_element_type=F32)
    h = _rms_norm(x1, g_ref[...]).astype(BF16)
    o_ref[...] = x1 + 0.5 * _swiglu(h, wg_ref, wu_ref, wd_ref)


def _out_ffn(x2d, attn2d, sga2d, rnn2d, layer, woa, wout, g, wg, wu, wd):
    n_tok = x2d.shape[0]
    tm = TM_OUT
    tile = pl.BlockSpec((tm, D_MODEL), lambda i: (i, 0))
    consts = (woa, wout, g, wg, wu, wd)
    return pl.pallas_call(
        _out_ffn_kernel,
        grid=(n_tok // tm,),
        in_specs=[tile, tile, tile, tile] + [_param_spec(a, layer) for a in consts],
        out_specs=tile,
        out_shape=jax.ShapeDtypeStruct((n_tok, D_MODEL), F32),
        compiler_params=pltpu.CompilerParams(
            dimension_semantics=("parallel",), vmem_limit_bytes=_vmem_limit(VMEM_LARGE_CALLS_MIB)),
        name="out_ffn",
    )(x2d, attn2d, sga2d, rnn2d, *consts)


def _block_diag_tiles(w):
    per = V7X_MXU_DIM // RNN_BLOCK
    w4 = w.reshape(w.shape[0], N_RNN_BLOCKS // per, per, RNN_BLOCK, RNN_BLOCK)
    eye = jnp.eye(per, dtype=w.dtype)
    return jnp.einsum("ljarc,ab->ljarbc", w4, eye).reshape(
        w.shape[0], -1, V7X_MXU_DIM, V7X_MXU_DIM)


def _bias_selector():
    r = jnp.arange(V7X_LANES)[:, None]
    col = jnp.arange(N_HEADS * V7X_LANES)[None, :]
    head, pos = col // V7X_LANES, col % V7X_LANES
    ones_row = N_SPLIT * N_HEADS + head
    piece_row = (pos - N_SPLIT) * N_HEADS + head
    hit = jnp.where(pos < N_SPLIT, r == ones_row, (pos < 2 * N_SPLIT) & (r == piece_row))
    return hit.astype(BF16)


def kernel(x, ffn1_norm, ffn1_w_gate, ffn1_w_up, ffn1_w_down, mix_norm, w_in, b_in, q_norm, k_norm, conv_w, conv_b, lru_w_a, lru_b_a, lru_w_x, lru_b_x, lru_lambda, w_o_attn, w_o_rnn, w_out, ffn2_norm, ffn2_w_gate, ffn2_w_up, ffn2_w_down):
    bsz, seq, _ = x.shape
    depth = w_in.shape[0]
    n_tok = bsz * seq
    assert seq % T_ATTN == 0 and seq % TM_RNN == 0 and n_tok % TM_FFN == 0 and n_tok % TM_OUT == 0
    row = lambda v: v.reshape(depth, 1, -1).astype(F32)
    bf16 = lambda w: w.astype(BF16)
    sel = _bias_selector()
    f_lo, f_hi = 3 * D_ATTN, 3 * D_ATTN + N_HEADS
    ngrp = D_RNN // V7X_MXU_DIM

    ffn1 = (row(ffn1_norm), bf16(ffn1_w_gate), bf16(ffn1_w_up), bf16(ffn1_w_down))
    wft = jnp.zeros((depth, 2 * V7X_SUBLANES, D_MODEL), BF16).at[:, :N_HEADS].set(
        bf16(jnp.swapaxes(w_in[:, :, f_lo:f_hi], 1, 2)))
    attn_w = (row(mix_norm), bf16(w_in[:, :, :f_lo]), row(b_in[:, :f_lo]), wft,
              b_in[:, f_lo:f_hi].reshape(depth, N_HEADS, 1), row(q_norm), row(k_norm))
    wr, br = bf16(w_in[:, :, f_hi:]), b_in[:, f_hi:]
    wxg = wr[:, :, :2 * D_RNN].reshape(depth, D_MODEL, 2, ngrp, V7X_MXU_DIM).transpose(
        0, 3, 1, 2, 4).reshape(depth, ngrp, D_MODEL, 2 * V7X_MXU_DIM)
    bxg = br[:, :2 * D_RNN].reshape(depth, 2, ngrp, V7X_MXU_DIM).transpose(0, 2, 1, 3).reshape(
        depth, ngrp, 1, 2 * V7X_MXU_DIM)
    wgate = bf16(jnp.concatenate([_block_diag_tiles(lru_w_a), _block_diag_tiles(lru_w_x)], axis=3))
    rnn_w = (wxg, bxg, wr[:, :, 2 * D_RNN:], row(br[:, 2 * D_RNN:]), conv_w.astype(F32),
             row(conv_b), wgate, row(lru_b_a), row(lru_b_x), row(lru_lambda), bf16(w_o_rnn))
    out_w = (bf16(w_o_attn), bf16(w_out), row(ffn2_norm), bf16(ffn2_w_gate), bf16(ffn2_w_up),
             bf16(ffn2_w_down))

    for l in range(depth):
        x = _ffn(x.reshape(n_tok, D_MODEL), l, *ffn1).reshape(bsz, seq, D_MODEL)
        qt, kaug, vt, c = _attn_proj(x, l, *attn_w, sel)
        sga, rnn = _rnn_branch(x, l, attn_w[0], *rnn_w)
        attn = _fox_attn(qt, c, kaug, vt)
        x = _out_ffn(
            x.reshape(n_tok, D_MODEL), attn.reshape(n_tok, D_ATTN), sga.reshape(n_tok, D_MODEL),
            rnn.reshape(n_tok, D_MODEL), l, *out_w).reshape(bsz, seq, D_MODEL)
    return x
```

```python
import jax
import jax.numpy as jnp
from jax import lax
from jax.experimental import pallas as pl
from jax.experimental.pallas import tpu as pltpu

F32 = jnp.float32
BF16 = jnp.bfloat16

D_MODEL = 1024
N_HEADS = 8
HEAD_DIM = 128
D_ATTN = N_HEADS * HEAD_DIM
D_RNN = D_MODEL
N_RNN_BLOCKS = 16
RNN_BLOCK = D_RNN // N_RNN_BLOCKS
CONV_WIDTH = 4
LRU_C = 8.0
D_FF = 2816
NORM_EPS = 1e-6

V7X_LANES = 128
V7X_SUBLANES = 8
V7X_MXU_DIM = 256
V7X_VMEM_BYTES = 64 * 1024 * 1024

AUG_DIM = HEAD_DIM + V7X_LANES
N_SPLIT = 3

FF_CHUNKS = tuple((lo, min(lo + 512, D_FF)) for lo in range(0, D_FF, 512))

TM_FFN = 1024
TM_OUT = 512
TM_RNN = 512
T_ATTN = 512
KV_PER_Q = 4
TQ_ATTN = KV_PER_Q * T_ATTN
LONG_TRIP_Q = 2
SCRATCH_PAD = 2 * V7X_LANES
VT_ROWS = HEAD_DIM + 16

MASK_VALUE = -1e30
LOG2_E = 1.4426950408889634


MIB = 1024 * 1024
VMEM_TILE_CALLS_MIB = 48
VMEM_LARGE_CALLS_MIB = 56


def _vmem_limit(mib):
    assert mib * MIB < V7X_VMEM_BYTES
    return mib * MIB


def _param_spec(arr, layer=None):
    if layer is None:
        block, idx = arr.shape, (0,) * arr.ndim
    else:
        block, idx = (None,) + arr.shape[1:], (layer,) + (0,) * (arr.ndim - 1)
    return pl.BlockSpec(block, lambda *_: idx, pipeline_mode=pl.Buffered(1))


def _rms_norm(x, g):
    return x * lax.rsqrt(jnp.mean(x * x, axis=-1, keepdims=True) + NORM_EPS) * g


def _swiglu(h_bf, wg_ref, wu_ref, wd_ref):
    acc = None
    for lo, hi in FF_CHUNKS:
        g = jnp.dot(h_bf, wg_ref[:, lo:hi], preferred_element_type=F32)
        u = jnp.dot(h_bf, wu_ref[:, lo:hi], preferred_element_type=F32)
        a = (g * jax.nn.sigmoid(g) * u).astype(BF16)
        part = jnp.dot(a, wd_ref[lo:hi, :], preferred_element_type=F32)
        acc = part if acc is None else acc + part
    return acc


def _ffn_kernel(x_ref, g_ref, wg_ref, wu_ref, wd_ref, o_ref):
    x = x_ref[...]
    h = _rms_norm(x, g_ref[...]).astype(BF16)
    o_ref[...] = x + 0.5 * _swiglu(h, wg_ref, wu_ref, wd_ref)


def _ffn(x2d, layer, g, wg, wu, wd):
    n_tok = x2d.shape[0]
    tm = TM_FFN
    tile = pl.BlockSpec((tm, D_MODEL), lambda i: (i, 0))
    return pl.pallas_call(
        _ffn_kernel,
        grid=(n_tok // tm,),
        in_specs=[tile] + [_param_spec(a, layer) for a in (g, wg, wu, wd)],
        out_specs=tile,
        out_shape=jax.ShapeDtypeStruct((n_tok, D_MODEL), F32),
        compiler_params=pltpu.CompilerParams(
            dimension_semantics=("parallel",), vmem_limit_bytes=_vmem_limit(VMEM_TILE_CALLS_MIB)),
        name="ffn",
    )(x2d, g, wg, wu, wd)


def _split3(c):
    hi = c.astype(BF16).astype(F32)
    r = c - hi
    mid = r.astype(BF16).astype(F32)
    lo = (r - mid).astype(BF16).astype(F32)
    return hi, mid, lo


def _attn_proj_kernel(x_ref, g_ref, wqkv_ref, bqkv_ref, wft_ref, bf_ref, qn_ref, kn_ref, sel_ref,
                      qt_ref, kaug_ref, vt_ref, c_ref, c_carry):
    tm = x_ref.shape[0]

    @pl.when(pl.program_id(1) == 0)
    def _():
        c_carry[...] = jnp.zeros_like(c_carry)

    h = _rms_norm(x_ref[...], g_ref[...]).astype(BF16)

    fl = lax.dot_general(wft_ref[...], h, (((1,), (1,)), ((), ())), preferred_element_type=F32)
    fl = fl[:N_HEADS, :] + bf_ref[...]
    log_f = jnp.minimum(fl, 0.0) - jnp.log1p(jnp.exp(-jnp.abs(fl)))

    lane = lax.broadcasted_iota(jnp.int32, log_f.shape, 1)
    cs = log_f
    d = 1
    while d < tm:
        cs = cs + jnp.where(lane >= d, pltpu.roll(cs, d, axis=1), 0.0)
        d *= 2
    c = cs + c_carry[:, 0:1]
    c_carry[...] = jnp.broadcast_to(c[:, tm - 1:tm], c_carry.shape)
    c_ref[...] = c

    hi, mid, lo = _split3(c * LOG2_E)
    ones = jnp.ones_like(c)
    pad = jnp.zeros((V7X_LANES - (N_SPLIT + 1) * N_HEADS, tm), F32)
    stack = jnp.concatenate([-hi, -mid, -lo, ones, pad], axis=0)
    stack_t = stack.T.astype(BF16)

    qkv = jnp.dot(h, wqkv_ref[...], preferred_element_type=F32) + bqkv_ref[...]
    extras = jnp.dot(stack_t, sel_ref[...], preferred_element_type=F32)

    scale = HEAD_DIM ** -0.5 * LOG2_E
    for hh in range(N_HEADS):
        sl = slice(hh * HEAD_DIM, (hh + 1) * HEAD_DIM)
        qh = qkv[:, sl]
        qh = qh * lax.rsqrt(jnp.mean(qh * qh, axis=-1, keepdims=True) + NORM_EPS) * qn_ref[...]
        qt_ref[hh] = (qh * scale).T.astype(BF16)
        kh = qkv[:, D_ATTN + hh * HEAD_DIM:D_ATTN + (hh + 1) * HEAD_DIM]
        kh = kh * lax.rsqrt(jnp.mean(kh * kh, axis=-1, keepdims=True) + NORM_EPS) * kn_ref[...]
        kaug_ref[hh, :, 0:HEAD_DIM] = kh.astype(BF16)
        kaug_ref[hh, :, HEAD_DIM:AUG_DIM] = extras[:, sl].astype(BF16)
        vh = qkv[:, 2 * D_ATTN + hh * HEAD_DIM:2 * D_ATTN + (hh + 1) * HEAD_DIM]
        vt_ref[hh, 0:HEAD_DIM, :] = vh.T.astype(BF16)
        vt_ref[hh, HEAD_DIM:VT_ROWS, :] = jnp.ones((VT_ROWS - HEAD_DIM, tm), BF16)


def _attn_proj(x, layer, g, wqkv, bqkv, wft, bf, qn, kn, sel):
    bsz, seq, _ = x.shape
    tm = T_ATTN
    n_t = seq // tm
    out_shape = (
        jax.ShapeDtypeStruct((bsz, N_HEADS, n_t, HEAD_DIM, tm), BF16),
        jax.ShapeDtypeStruct((bsz, N_HEADS, seq, AUG_DIM), BF16),
        jax.ShapeDtypeStruct((bsz, N_HEADS, n_t, VT_ROWS, tm), BF16),
        jax.ShapeDtypeStruct((bsz, N_HEADS, seq), F32),
    )
    out_specs = (
        pl.BlockSpec((None, N_HEADS, None, HEAD_DIM, tm), lambda b, t: (b, 0, t, 0, 0)),
        pl.BlockSpec((None, N_HEADS, tm, AUG_DIM), lambda b, t: (b, 0, t, 0)),
        pl.BlockSpec((None, N_HEADS, None, VT_ROWS, tm), lambda b, t: (b, 0, t, 0, 0)),
        pl.BlockSpec((None, N_HEADS, tm), lambda b, t: (b, 0, t)),
    )
    stacked = (g, wqkv, bqkv, wft, bf, qn, kn)
    consts = stacked + (sel,)
    return pl.pallas_call(
        _attn_proj_kernel,
        grid=(bsz, n_t),
        in_specs=[pl.BlockSpec((None, tm, D_MODEL), lambda b, t: (b, t, 0))]
        + [_param_spec(a, layer) for a in stacked] + [_param_spec(sel)],
        out_specs=out_specs,
        out_shape=out_shape,
        scratch_shapes=[pltpu.VMEM((N_HEADS, V7X_LANES), F32)],
        compiler_params=pltpu.CompilerParams(
            dimension_semantics=("arbitrary", "arbitrary"), vmem_limit_bytes=_vmem_limit(VMEM_TILE_CALLS_MIB)),
        name="attn_proj",
    )(x, *consts)


def _gelu_tanh(x):
    return 0.5 * x * (1.0 + jnp.tanh(0.7978845608028654 * (x + 0.044715 * (x * x * x))))


def _sigmoid(x):
    return 0.5 * jnp.tanh(0.5 * x) + 0.5


def _rnn_kernel(x_ref, g_ref, perm_ref, unperm_ref, wxg_ref, bxg_ref, wm_ref, bm_ref, cw_ref,
                cb_ref, wgate_ref, ba_ref, bx_ref, lam_ref, wo_ref, sga_ref, rnn_ref, tail, h_carry):
    tm = x_ref.shape[0]
    sub = V7X_SUBLANES
    gw = V7X_MXU_DIM
    ngrp = D_RNN // gw
    nv = tm // sub
    ntail = CONV_WIDTH - 1
    mw = 2 * D_MODEL // ngrp

    @pl.when(pl.program_id(1) == 0)
    def _():
        tail[...] = jnp.zeros_like(tail)
        h_carry[...] = jnp.zeros_like(h_carry)

    sub_id = lax.broadcasted_iota(jnp.int32, (sub, gw), 0)
    first = sub_id == 0

    def vreg(arr, r):
        return arr[r * sub:(r + 1) * sub, :]

    h = _rms_norm(x_ref[...], g_ref[...]).astype(BF16)
    hp = jnp.dot(perm_ref[...], h, preferred_element_type=F32).astype(BF16)

    def xg_proj(j):
        return jnp.dot(hp, wxg_ref[j], preferred_element_type=F32) + bxg_ref[j]

    def merge_gate(j):
        lo, hi = j * mw, (j + 1) * mw
        return _sigmoid(jnp.dot(h, wm_ref[:, lo:hi], preferred_element_type=F32) + bm_ref[:, lo:hi])

    def conv_stage(j, xr):
        lo, hi = j * gw, (j + 1) * gw
        conv = cb_ref[:, lo:hi]
        for tap in range(CONV_WIDTH):
            k = CONV_WIDTH - 1 - tap
            if k == 0:
                shifted = xr
            else:
                head = []
                for i in range(k):
                    prev = pltpu.roll(tail[(ntail - k + i) * sub:(ntail - k + i + 1) * sub, lo:hi],
                                      1, axis=0)
                    cur = pltpu.roll(vreg(xr, nv - k + i), 1, axis=0)
                    head.append(jnp.where(first, prev, cur))
                shifted = jnp.concatenate(head + [xr[:(nv - k) * sub, :]], axis=0)
            conv = conv + shifted * cw_ref[tap:tap + 1, lo:hi]
        tail[:, lo:hi] = xr[(nv - ntail) * sub:, :]
        return conv

    def lru_stage(j, gz, conv, gr):
        lo, hi = j * gw, (j + 1) * gw
        r_gate = _sigmoid(gz[:, :gw] + ba_ref[:, lo:hi])
        i_gate = _sigmoid(gz[:, gw:] + bx_ref[:, lo:hi])
        nlam = -lam_ref[:, lo:hi]
        softplus = jnp.maximum(nlam, 0.0) + jnp.log1p(jnp.exp(-jnp.abs(nlam)))
        log_a = (-LRU_C * softplus) * r_gate
        a = jnp.exp(log_a)
        om = -jnp.tanh(log_a) * (a * a + 1.0)
        u = jnp.where(om == 0.0, 0.0, om * lax.rsqrt(om)) * (i_gate * conv)

        h_loc, a_cum = [vreg(u, 0)], [vreg(a, 0)]
        for r in range(1, nv):
            ar = vreg(a, r)
            h_loc.append(ar * h_loc[-1] + vreg(u, r))
            a_cum.append(ar * a_cum[-1])
        aa, bb = a_cum[-1], h_loc[-1]
        d = 1
        while d < sub:
            keep = sub_id >= d
            a_sh = jnp.where(keep, pltpu.roll(aa, d, axis=0), 1.0)
            b_sh = jnp.where(keep, pltpu.roll(bb, d, axis=0), 0.0)
            bb = aa * b_sh + bb
            aa = aa * a_sh
            d *= 2
        h_in = h_carry[:, lo:hi]
        chunk_end = aa * h_in + bb
        entry = jnp.where(first, h_in, pltpu.roll(chunk_end, 1, axis=0))
        h_carry[:, lo:hi] = jnp.broadcast_to(chunk_end[sub - 1:sub, :], (sub, gw))
        hs = jnp.concatenate([h_loc[r] + a_cum[r] * entry for r in range(nv)], axis=0)
        return (hs * _gelu_tanh(gr)).astype(BF16)

    def out_stage(j, yr):
        yr = jnp.dot(unperm_ref[...], yr, preferred_element_type=F32).astype(BF16)
        return jnp.dot(yr, wo_ref[j * gw:(j + 1) * gw, :], preferred_element_type=F32)

    def gate_stage(j, conv):
        return jnp.dot(conv.astype(BF16), wgate_ref[j], preferred_element_type=F32)

    mg = {0: merge_gate(0)}
    rest = {0: xg_proj(0), 1: xg_proj(1)}
    conv = {0: conv_stage(0, rest[0][:, :gw])}
    gz = {0: gate_stage(0, conv[0])}
    mg[1] = merge_gate(1)
    rest[2] = xg_proj(2)
    conv[1] = conv_stage(1, rest[1][:, :gw])
    yr = {0: lru_stage(0, gz[0], conv[0], rest[0][:, gw:])}
    gz[1] = gate_stage(1, conv[1])
    y_rnn = out_stage(0, yr[0])
    rest[3] = xg_proj(3)
    conv[2] = conv_stage(2, rest[2][:, :gw])
    yr[1] = lru_stage(1, gz[1], conv[1], rest[1][:, gw:])
    gz[2] = gate_stage(2, conv[2])
    y_rnn = y_rnn + out_stage(1, yr[1])
    conv[3] = conv_stage(3, rest[3][:, :gw])
    yr[2] = lru_stage(2, gz[2], conv[2], rest[2][:, gw:])
    gz[3] = gate_stage(3, conv[3])
    mg[2] = merge_gate(2)
    y_rnn = y_rnn + out_stage(2, yr[2])
    yr[3] = lru_stage(3, gz[3], conv[3], rest[3][:, gw:])
    mg[3] = merge_gate(3)
    y_rnn = y_rnn + out_stage(3, yr[3])

    sga_ref[...] = jnp.concatenate([mg[0], mg[1]], axis=1)
    rnn_ref[...] = jnp.concatenate([mg[2], mg[3]], axis=1) * y_rnn


def _time_permutation(tm):
    rho = jnp.arange(tm)
    src = (tm // V7X_SUBLANES) * (rho % V7X_SUBLANES) + rho // V7X_SUBLANES
    return (src[:, None] == jnp.arange(tm)[None, :]).astype(BF16)


def _rnn_branch(x, layer, g, wxg, bxg, wm, bm, cw, cb, wgate, ba, bx, lam, wo):
    bsz, seq, _ = x.shape
    tm = TM_RNN
    tile = pl.BlockSpec((None, tm, D_MODEL), lambda b, t: (b, t, 0))
    perm = _time_permutation(tm)
    shared = (perm, perm.T)
    stacked = (wxg, bxg, wm, bm, cw, cb, wgate, ba, bx, lam, wo)
    consts = (g,) + shared + stacked
    out = jax.ShapeDtypeStruct((bsz, seq, D_MODEL), F32)
    return pl.pallas_call(
        _rnn_kernel,
        grid=(bsz, seq // tm),
        in_specs=[tile, _param_spec(g, layer)] + [_param_spec(a) for a in shared]
        + [_param_spec(a, layer) for a in stacked],
        out_specs=(tile, tile),
        out_shape=(out, out),
        scratch_shapes=[
            pltpu.VMEM(((CONV_WIDTH - 1) * V7X_SUBLANES, D_RNN), F32),
            pltpu.VMEM((V7X_SUBLANES, D_RNN), F32),
        ],
        compiler_params=pltpu.CompilerParams(
            dimension_semantics=("arbitrary", "arbitrary"), vmem_limit_bytes=_vmem_limit(VMEM_TILE_CALLS_MIB)),
        name="rnn_branch",
    )(x, *consts)


def _fox_attn_kernel(qt_ref, c_ref, k_ref, vt_ref, o_ref, qt_aug, st0, st1, p0, p1, al0, al1,
                     bm0, bm1, m_s, acc):
    tq = o_ref.shape[0]
    tk = vt_ref.shape[2]
    hh = pl.program_id(1)
    qi = pl.program_id(2)
    st, pb, al, bm = (st0, st1), (p0, p1), (al0, al1), (bm0, bm1)

    for ci in range(qt_ref.shape[0]):
        qt_aug[0:HEAD_DIM, ci * tk:(ci + 1) * tk] = qt_ref[ci]
    hi, mid, lo = _split3(c_ref[pl.ds(hh, 1), :] * LOG2_E)
    row = lax.broadcasted_iota(jnp.int32, (V7X_SUBLANES, tq), 0)
    top = jnp.where(row == 0, hi, jnp.where(row == 1, mid, jnp.where(row == 2, lo,
          jnp.where(row < 2 * N_SPLIT, 1.0, 0.0))))
    qt_aug[HEAD_DIM:AUG_DIM, :] = jnp.concatenate(
        [top, jnp.zeros((V7X_LANES - V7X_SUBLANES, tq), F32)], axis=0).astype(BF16)

    m_s[...] = jnp.full(m_s.shape, MASK_VALUE, F32)
    acc[...] = jnp.zeros_like(acc)

    @pl.when(qi == 0)
    def _():
        p1[...] = jnp.zeros_like(p1)
        al1[...] = jnp.ones_like(al1)

    def score(j, s, c0=0, with_max=True):
        kb = k_ref[pl.ds(pl.multiple_of(j * tk, tk), tk), :]
        blk = jnp.dot(kb, qt_aug[:, c0:], preferred_element_type=F32)
        st[s][:, c0:tq] = blk
        if with_max:
            bm[s][...] = jnp.max(blk, axis=0, keepdims=True)

    def softmax(s, c0=0, diagonal=False):
        st_ref = st[s]
        if diagonal:
            shape = (tk, tq - c0)
            visible = (lax.broadcasted_iota(jnp.int32, shape, 0)
                       <= lax.broadcasted_iota(jnp.int32, shape, 1))
            st_ref[:, c0:tq] = jnp.where(visible, st_ref[:, c0:tq], MASK_VALUE)
            blk_max = jnp.max(st_ref[:, c0:tq], axis=0, keepdims=True)
        else:
            blk_max = bm[s][...]
        m_old = m_s[:, c0:]
        m_new = jnp.maximum(m_old, blk_max)
        m_s[:, c0:] = m_new
        al[s][:, c0:] = jnp.exp2(m_old - m_new)
        pb[s][:, c0:tq] = jnp.exp2(st_ref[:, c0:tq] - m_new).astype(BF16)

    def pv(j, s, c0=0):
        acc[:, c0:tq] = al[s][:, c0:] * acc[:, c0:tq] + jnp.dot(
            vt_ref[j], pb[s][:, c0:tq], preferred_element_type=F32)

    def visible_blocks(a, nblk, first=False):
        score(a, 0)
        score(a + 1, 1)
        for d in range(nblk):
            s = d % 2
            softmax(s)
            if not (first and d == 0):
                pv(a + d - 1, 1 - s)
            if d + 2 < nblk:
                score(a + d + 2, s)

    @pl.when(qi > 0)
    def _():
        visible_blocks(jnp.int32(0), KV_PER_Q, first=True)

    long_blk = LONG_TRIP_Q * KV_PER_Q
    rest_q = jnp.maximum(qi - 1, 0)
    n_long = rest_q // LONG_TRIP_Q

    def long_trip(t, carry):
        visible_blocks(KV_PER_Q + long_blk * t, long_blk)
        return carry

    def short_trip(t, carry):
        visible_blocks(KV_PER_Q + long_blk * n_long + KV_PER_Q * t, KV_PER_Q)
        return carry

    lax.fori_loop(0, n_long, long_trip, 0)
    lax.fori_loop(0, rest_q - LONG_TRIP_Q * n_long, short_trip, 0)
    a = KV_PER_Q * qi
    score(a, 0, with_max=False)
    score(a + 1, 1, c0=tk, with_max=False)
    for d in range(KV_PER_Q):
        s = d % 2
        softmax(s, c0=d * tk, diagonal=True)
        pv(jnp.maximum(a + d - 1, 0), 1 - s, c0=max(d - 1, 0) * tk)
        if d + 2 < KV_PER_Q:
            score(a + d + 2, s, c0=(d + 2) * tk, with_max=False)
    pv(a + KV_PER_Q - 1, (KV_PER_Q - 1) % 2, c0=(KV_PER_Q - 1) * tk)
    o_ref[...] = (acc[0:HEAD_DIM, 0:tq] / acc[HEAD_DIM:HEAD_DIM + 1, 0:tq]).T.astype(BF16)


def _fox_attn(qt, c, kaug, vt):
    bsz, seq = c.shape[0], c.shape[2]
    n_t, tk = vt.shape[2], vt.shape[4]
    tq = TQ_ATTN
    assert tq == KV_PER_Q * tk and KV_PER_Q % 2 == 0 and seq % tq == 0
    return pl.pallas_call(
        _fox_attn_kernel,
        grid=(bsz, N_HEADS, seq // tq),
        in_specs=[
            pl.BlockSpec((None, None, KV_PER_Q, HEAD_DIM, tk), lambda b, h, i: (b, h, i, 0, 0)),
            pl.BlockSpec((None, N_HEADS, tq), lambda b, h, i: (b, 0, i)),
            pl.BlockSpec((None, None, seq, AUG_DIM), lambda b, h, i: (b, h, 0, 0)),
            pl.BlockSpec((None, None, n_t, VT_ROWS, tk), lambda b, h, i: (b, h, 0, 0, 0)),
        ],
        out_specs=pl.BlockSpec((None, tq, HEAD_DIM), lambda b, h, i: (b, i, h)),
        out_shape=jax.ShapeDtypeStruct((bsz, seq, D_ATTN), BF16),
        scratch_shapes=[
            pltpu.VMEM((AUG_DIM, tq), BF16),
            pltpu.VMEM((tk, tq + SCRATCH_PAD), F32),
            pltpu.VMEM((tk, tq + SCRATCH_PAD), F32),
            pltpu.VMEM((tk, tq + SCRATCH_PAD), BF16),
            pltpu.VMEM((tk, tq + SCRATCH_PAD), BF16),
            pltpu.VMEM((1, tq), F32),
            pltpu.VMEM((1, tq), F32),
            pltpu.VMEM((1, tq), F32),
            pltpu.VMEM((1, tq), F32),
            pltpu.VMEM((1, tq), F32),
            pltpu.VMEM((VT_ROWS, tq + SCRATCH_PAD), F32),
        ],
        compiler_params=pltpu.CompilerParams(
            dimension_semantics=("arbitrary", "arbitrary", "arbitrary"),
            vmem_limit_bytes=_vmem_limit(VMEM_LARGE_CALLS_MIB)),
        name="fox_attn",
    )(qt, c, kaug, vt)


def _out_ffn_kernel(x_ref, attn_ref, sga_ref, rnn_ref, woa_ref, wout_ref, g_ref, wg_ref, wu_ref,
                    wd_ref, o_ref):
    y_attn = jnp.dot(attn_ref[...], woa_ref[...], preferred_element_type=F32)
    merged = sga_ref[...] * y_attn + rnn_ref[...]
    x1 = x_ref[...] + jnp.dot(merged.astype(BF16), wout_ref[...], preferred_element_type=F32)
    h = _rms_norm(x1, g_ref[...]).astype(BF16)
    o_ref[...] = x1 + 0.5 * _swiglu(h, wg_ref, wu_ref, wd_ref)


def _out_ffn(x2d, attn2d, sga2d, rnn2d, layer, woa, wout, g, wg, wu, wd):
    n_tok = x2d.shape[0]
    tm = TM_OUT
    tile = pl.BlockSpec((tm, D_MODEL), lambda i: (i, 0))
    consts = (woa, wout, g, wg, wu, wd)
    return pl.pallas_call(
        _out_ffn_kernel,
        grid=(n_tok // tm,),
        in_specs=[tile, tile, tile, tile] + [_param_spec(a, layer) for a in consts],
        out_specs=tile,
        out_shape=jax.ShapeDtypeStruct((n_tok, D_MODEL), F32),
        compiler_params=pltpu.CompilerParams(
            dimension_semantics=("parallel",), vmem_limit_bytes=_vmem_limit(VMEM_LARGE_CALLS_MIB)),
        name="out_ffn",
    )(x2d, attn2d, sga2d, rnn2d, *consts)


def _block_diag_tiles(w):
    per = V7X_MXU_DIM // RNN_BLOCK
    w4 = w.reshape(w.shape[0], N_RNN_BLOCKS // per, per, RNN_BLOCK, RNN_BLOCK)
    eye = jnp.eye(per, dtype=w.dtype)
    return jnp.einsum("ljarc,ab->ljarbc", w4, eye).reshape(
        w.shape[0], -1, V7X_MXU_DIM, V7X_MXU_DIM)


def _bias_selector():
    r = jnp.arange(V7X_LANES)[:, None]
    col = jnp.arange(N_HEADS * V7X_LANES)[None, :]
    head, pos = col // V7X_LANES, col % V7X_LANES
    ones_row = N_SPLIT * N_HEADS + head
    piece_row = (pos - N_SPLIT) * N_HEADS + head
    hit = jnp.where(pos < N_SPLIT, r == ones_row, (pos < 2 * N_SPLIT) & (r == piece_row))
    return hit.astype(BF16)


def kernel(x, ffn1_norm, ffn1_w_gate, ffn1_w_up, ffn1_w_down, mix_norm, w_in, b_in, q_norm, k_norm, conv_w, conv_b, lru_w_a, lru_b_a, lru_w_x, lru_b_x, lru_lambda, w_o_attn, w_o_rnn, w_out, ffn2_norm, ffn2_w_gate, ffn2_w_up, ffn2_w_down):
    bsz, seq, _ = x.shape
    depth = w_in.shape[0]
    n_tok = bsz * seq
    assert seq % T_ATTN == 0 and seq % TM_RNN == 0 and n_tok % TM_FFN == 0 and n_tok % TM_OUT == 0
    row = lambda v: v.reshape(depth, 1, -1).astype(F32)
    bf16 = lambda w: w.astype(BF16)
    sel = _bias_selector()
    f_lo, f_hi = 3 * D_ATTN, 3 * D_ATTN + N_HEADS
    ngrp = D_RNN // V7X_MXU_DIM

    ffn1 = (row(ffn1_norm), bf16(ffn1_w_gate), bf16(ffn1_w_up), bf16(ffn1_w_down))
    wft = jnp.zeros((depth, 2 * V7X_SUBLANES, D_MODEL), BF16).at[:, :N_HEADS].set(
        bf16(jnp.swapaxes(w_in[:, :, f_lo:f_hi], 1, 2)))
    attn_w = (row(mix_norm), bf16(w_in[:, :, :f_lo]), row(b_in[:, :f_lo]), wft,
              b_in[:, f_lo:f_hi].reshape(depth, N_HEADS, 1), row(q_norm), row(k_norm))
    wr, br = bf16(w_in[:, :, f_hi:]), b_in[:, f_hi:]
    wxg = wr[:, :, :2 * D_RNN].reshape(depth, D_MODEL, 2, ngrp, V7X_MXU_DIM).transpose(
        0, 3, 1, 2, 4).reshape(depth, ngrp, D_MODEL, 2 * V7X_MXU_DIM)
    bxg = br[:, :2 * D_RNN].reshape(depth, 2, ngrp, V7X_MXU_DIM).transpose(0, 2, 1, 3).reshape(
        depth, ngrp, 1, 2 * V7X_MXU_DIM)
    wgate = bf16(jnp.concatenate([_block_diag_tiles(lru_w_a), _block_diag_tiles(lru_w_x)], axis=3))
    rnn_w = (wxg, bxg, wr[:, :, 2 * D_RNN:], row(br[:, 2 * D_RNN:]), conv_w.astype(F32),
             row(conv_b), wgate, row(lru_b_a), row(lru_b_x), row(lru_lambda), bf16(w_o_rnn))
    out_w = (bf16(w_o_attn), bf16(w_out), row(ffn2_norm), bf16(ffn2_w_gate), bf16(ffn2_w_up),
             bf16(ffn2_w_down))

    for l in range(depth):
        x = _ffn(x.reshape(n_tok, D_MODEL), l, *ffn1).reshape(bsz, seq, D_MODEL)
        qt, kaug, vt, c = _attn_proj(x, l, *attn_w, sel)
        sga, rnn = _rnn_branch(x, l, attn_w[0], *rnn_w)
        attn = _fox_attn(qt, c, kaug, vt)
        x = _out_ffn(
            x.reshape(n_tok, D_MODEL), attn.reshape(n_tok, D_ATTN), sga.reshape(n_tok, D_MODEL),
            rnn.reshape(n_tok, D_MODEL), l, *out_w).reshape(bsz, seq, D_MODEL)
    return x
```
